```python
import math
import jax, jax.numpy as jnp
from jax import lax
import numpy as np

D_MODEL = 1024
BATCH = 4
SEQ = 4096
DEPTH = 2

D_MIX = D_MODEL
W_CONV = D_MIX // 4
W_GLA = D_MIX // 4
W_MLA = D_MIX // 4
W_CFM = D_MIX // 4
SC_WIDTH = 3
GLA_HEADS = 4
GLA_DV = W_GLA // GLA_HEADS
GLA_DK = GLA_DV // 2
GLA_LOWRANK = 16
GLA_TAU = 16.0
GLA_CHUNK = 64
MLA_HEADS = 4
MLA_V_DIM = W_MLA // MLA_HEADS
MLA_NOPE = 64
MLA_ROPE = 32
MLA_Q_RANK = 256
MLA_KV_RANK = 128
ROPE_BASE = 10000.0
Q_BLOCK = 128
CFM_WIDTH = 31
D_FF = 2816
FFN_CONV_WIDTH = 3
DN_ALPHA = (2.0 * DEPTH) ** 0.25
DN_BETA = (8.0 * DEPTH) ** -0.25
EPS = 1e-5

SPLIT_SIZES = (
    W_CONV, W_CONV, W_CONV,
    GLA_HEADS * GLA_DK, GLA_HEADS * GLA_DK, W_GLA, W_GLA,
    GLA_LOWRANK, GLA_LOWRANK,
    MLA_Q_RANK, MLA_KV_RANK, MLA_ROPE,
    2 * W_CFM,
)
D_IN = sum(SPLIT_SIZES)

kernel_name = 'hybrid_parallel_groups_deepnorm_encoder'


def _split_points():
    pts, acc = [], 0
    for s in SPLIT_SIZES[:-1]:
        acc += s
        pts.append(acc)
    return pts


def layer_norm(x, g, b):
    xf = x.astype(jnp.float32)
    mu = jnp.mean(xf, axis=-1, keepdims=True)
    var = jnp.mean(jnp.square(xf - mu), axis=-1, keepdims=True)
    return ((xf - mu) * lax.rsqrt(var + EPS) * g.astype(jnp.float32) + b.astype(jnp.float32)).astype(x.dtype)


def rms_norm(x, g):
    xf = x.astype(jnp.float32)
    return (xf * lax.rsqrt(jnp.mean(xf * xf, axis=-1, keepdims=True) + EPS) * g.astype(jnp.float32)).astype(x.dtype)


def depthwise_conv(x, w):
    k, c = w.shape
    return lax.conv_general_dilated(
        x, w[:, None, :].astype(x.dtype), window_strides=(1,), padding=[(k // 2, k // 2)],
        dimension_numbers=('NWC', 'WIO', 'NWC'), feature_group_count=c)


def rope_tables(positions):
    inv = ROPE_BASE ** (-jnp.arange(0, MLA_ROPE, 2, dtype=jnp.float32) / MLA_ROPE)
    ang = positions.astype(jnp.float32)[..., None] * inv
    return jnp.cos(ang), jnp.sin(ang)


def apply_rope(x, cos, sin):
    half = MLA_ROPE // 2
    xf = x.astype(jnp.float32)
    x1, x2 = xf[..., :half], xf[..., half:]
    c, s = cos[:, :, None, :], sin[:, :, None, :]
    return jnp.concatenate([x1 * c - x2 * s, x1 * s + x2 * c], axis=-1).astype(x.dtype)


def gla_scan(q, k, v, log_a, include_diag):
    bsz, nh, s, dk = q.shape
    dv = v.shape[-1]
    n = s // GLA_CHUNK

    def chunks(t):
        return t.astype(jnp.float32).reshape(bsz, nh, n, GLA_CHUNK, t.shape[-1]).transpose(2, 0, 1, 3, 4)

    idx = jnp.arange(GLA_CHUNK)
    mask = (idx[:, None] >= idx[None, :]) if include_diag else (idx[:, None] > idx[None, :])
    mask5 = mask[None, None, :, :, None]

    def step(state, inp):
        qc, kc, vc, ac = inp
        b = jnp.cumsum(ac, axis=2)
        o_inter = jnp.einsum('bhtk,bhkv->bhtv', qc * jnp.exp(b), state)
        rel = b[:, :, :, None, :] - b[:, :, None, :, :]
        decay = jnp.where(mask5, jnp.exp(jnp.where(mask5, rel, 0.0)), 0.0)
        scores = jnp.einsum('bhtk,bhsk,bhtsk->bhts', qc, kc, decay)
        o = o_inter + jnp.einsum('bhts,bhsv->bhtv', scores, vc)
        b_last = b[:, :, -1:, :]
        state = jnp.exp(b_last[:, :, 0, :])[..., None] * state + \
            jnp.einsum('bhsk,bhsv->bhkv', kc * jnp.exp(b_last - b), vc)
        return state, o

    init = jnp.zeros((bsz, nh, dk, dv), jnp.float32)
    _, o = lax.scan(step, init, (chunks(q), chunks(k), chunks(v), chunks(log_a)))
    return o.transpose(1, 2, 0, 3, 4).reshape(bsz, nh, s, dv)


def blocked_attention(q, k, v, scale):
    bsz, s, nh, d = q.shape
    nb = s // Q_BLOCK
    qb = q.reshape(bsz, nb, Q_BLOCK, nh, d).transpose(1, 0, 2, 3, 4)

    def blk(qi):
        sc = jnp.einsum('bqhd,bkhd->bhqk', qi, k).astype(jnp.float32) * scale
        p = jax.nn.softmax(sc, axis=-1)
        return jnp.einsum('bhqk,bkhd->bqhd', p.astype(v.dtype), v)

    o = lax.map(blk, qb)
    return o.transpose(1, 0, 2, 3, 4).reshape(bsz, s, nh, v.shape[-1])


def token_mixers(h, cos, sin, w_in, w_sc_conv, w_gla_a_up, b_gla_a, g_gla_head, g_mla_q, w_mla_uq,
                 g_mla_kv, w_mla_ukv, w_cfm_dw, g_cfm_ln, b_cfm_ln, g_branch, w_out):
    bsz, s, _ = h.shape
    proj = h @ w_in
    (sc_b, sc_c, sc_h, gq, gk, gv, gg, ga_f, ga_b, cq, ckv, kr, cfm_in) = jnp.split(proj, _split_points(), axis=-1)

    y_a = sc_b * depthwise_conv(sc_c * sc_h, w_sc_conv)

    def heads(t, d):
        return t.reshape(bsz, s, GLA_HEADS, d).transpose(0, 2, 1, 3)

    q = heads(gq, GLA_DK) * (GLA_DK ** -0.5)
    k = heads(gk, GLA_DK)
    v = heads(gv, GLA_DV)
    la_f = heads(jax.nn.log_sigmoid((ga_f @ w_gla_a_up[0] + b_gla_a[0]).astype(jnp.float32)) / GLA_TAU, GLA_DK)
    la_b = heads(jax.nn.log_sigmoid((ga_b @ w_gla_a_up[1] + b_gla_a[1]).astype(jnp.float32)) / GLA_TAU, GLA_DK)
    flip = lambda t: jnp.flip(t, axis=2)
    o_f = gla_scan(q, k, v, la_f, True)
    o_b = flip(gla_scan(flip(q), flip(k), flip(v), flip(la_b), False))
    o = (o_f + o_b).transpose(0, 2, 1, 3)
    o = rms_norm(o, g_gla_head.reshape(GLA_HEADS, GLA_DV)).reshape(bsz, s, W_GLA)
    y_b = o.astype(h.dtype) * jax.nn.silu(gg)

    qm = (rms_norm(cq, g_mla_q) @ w_mla_uq).reshape(bsz, s, MLA_HEADS, MLA_NOPE + MLA_ROPE)
    q_c = jnp.concatenate([qm[..., :MLA_NOPE], apply_rope(qm[..., MLA_NOPE:], cos, sin)], axis=-1)
    kvm = (rms_norm(ckv, g_mla_kv) @ w_mla_ukv).reshape(bsz, s, MLA_HEADS, MLA_NOPE + MLA_V_DIM)
    k_rope = jnp.broadcast_to(apply_rope(kr[:, :, None, :], cos, sin), (bsz, s, MLA_HEADS, MLA_ROPE))
    k_c = jnp.concatenate([kvm[..., :MLA_NOPE], k_rope], axis=-1)
    v_c = kvm[..., MLA_NOPE:]
    y_c = blocked_attention(q_c, k_c, v_c, (MLA_NOPE + MLA_ROPE) ** -0.5).reshape(bsz, s, W_MLA)

    a, gate = jnp.split(cfm_in, 2, axis=-1)
    u = depthwise_conv(a * jax.nn.sigmoid(gate), w_cfm_dw)
    y_d = jax.nn.silu(layer_norm(u, g_cfm_ln, b_cfm_ln))

    gains = jnp.split(g_branch, [W_CONV, W_CONV + W_GLA, W_CONV + W_GLA + W_MLA])
    y = jnp.concatenate([rms_norm(yb, gb) for yb, gb in zip((y_a, y_b, y_c, y_d), gains)], axis=-1)
    return y @ w_out


def conv_ffn(h, w_up, w_dw, w_down):
    u = depthwise_conv(h @ w_up, w_dw)
    g, val = jnp.split(u, 2, axis=-1)
    return (jax.nn.silu(g) * val) @ w_down


def setup_inputs(seed: int = 0) -> dict:
    key = jax.random.key(seed)
    ks = jax.random.split(key, 32)
    L = DEPTH
    nrm = lambda k, shape, scale: jax.random.normal(k, shape, jnp.float32) * scale
    gain = lambda k, shape: 1.0 + 0.02 * jax.random.normal(k, shape, jnp.float32)
    offsets = jax.random.randint(ks[1], (BATCH, 1), 0, SEQ, dtype=jnp.int32)
    positions = jnp.arange(SEQ, dtype=jnp.int32)[None, :] + offsets
    return {
        'x': jax.random.normal(ks[0], (BATCH, SEQ, D_MODEL), jnp.float32),
        'positions': positions,
        'ln_in_g': gain(ks[2], (D_MODEL,)),
        'ln_in_b': nrm(ks[3], (D_MODEL,), 0.02),
        'w_in': nrm(ks[4], (L, D_MODEL, D_IN), D_MODEL ** -0.5),
        'w_sc_conv': nrm(ks[5], (L, SC_WIDTH, W_CONV), SC_WIDTH ** -0.5),
        'w_gla_a_up': nrm(ks[6], (L, 2, GLA_LOWRANK, GLA_HEADS * GLA_DK), GLA_LOWRANK ** -0.5),
        'b_gla_a': nrm(ks[7], (L, 2, GLA_HEADS * GLA_DK), 0.02),
        'g_gla_head': gain(ks[8], (L, W_GLA)),
        'g_mla_q': gain(ks[9], (L, MLA_Q_RANK)),
        'w_mla_uq': nrm(ks[10], (L, MLA_Q_RANK, MLA_HEADS * (MLA_NOPE + MLA_ROPE)), MLA_Q_RANK ** -0.5),
        'g_mla_kv': gain(ks[11], (L, MLA_KV_RANK)),
        'w_mla_ukv': nrm(ks[12], (L, MLA_KV_RANK, MLA_HEADS * (MLA_NOPE + MLA_V_DIM)), MLA_KV_RANK ** -0.5),
        'w_cfm_dw': nrm(ks[13], (L, CFM_WIDTH, W_CFM), CFM_WIDTH ** -0.5),
        'g_cfm_ln': gain(ks[14], (L, W_CFM)),
        'b_cfm_ln': nrm(ks[15], (L, W_CFM), 0.02),
        'g_branch': gain(ks[16], (L, D_MIX)),
        'w_out': nrm(ks[17], (L, D_MIX, D_MODEL), DN_BETA * D_MIX ** -0.5),
        'ln1_g': gain(ks[18], (L, D_MODEL)),
        'ln1_b': nrm(ks[19], (L, D_MODEL), 0.02),
        'w_ffn_up': nrm(ks[20], (L, D_MODEL, 2 * D_FF), D_MODEL ** -0.5),
        'w_ffn_dw': nrm(ks[21], (L, FFN_CONV_WIDTH, 2 * D_FF), FFN_CONV_WIDTH ** -0.5),
        'w_ffn_down': nrm(ks[22], (L, D_FF, D_MODEL), DN_BETA * D_FF ** -0.5),
        'ln2_g': gain(ks[23], (L, D_MODEL)),
        'ln2_b': nrm(ks[24], (L, D_MODEL), 0.02),
    }


def reference(x, positions, ln_in_g, ln_in_b, w_in, w_sc_conv, w_gla_a_up, b_gla_a, g_gla_head, g_mla_q,
              w_mla_uq, g_mla_kv, w_mla_ukv, w_cfm_dw, g_cfm_ln, b_cfm_ln, g_branch, w_out, ln1_g, ln1_b,
              w_ffn_up, w_ffn_dw, w_ffn_down, ln2_g, ln2_b):
    cos, sin = rope_tables(positions)
    h = layer_norm(x, ln_in_g, ln_in_b)
    for l in range(DEPTH):
        mix = token_mixers(h, cos, sin, w_in[l], w_sc_conv[l], w_gla_a_up[l], b_gla_a[l], g_gla_head[l],
                           g_mla_q[l], w_mla_uq[l], g_mla_kv[l], w_mla_ukv[l], w_cfm_dw[l], g_cfm_ln[l],
                           b_cfm_ln[l], g_branch[l], w_out[l])
        h = layer_norm(DN_ALPHA * h + mix, ln1_g[l], ln1_b[l])
        h = layer_norm(DN_ALPHA * h + conv_ffn(h, w_ffn_up[l], w_ffn_dw[l], w_ffn_down[l]), ln2_g[l], ln2_b[l])
    return h
```

```python
import functools
import math

import jax
import jax.numpy as jnp
from jax import lax
from jax.experimental import pallas as pl
from jax.experimental.pallas import tpu as pltpu

F32 = jnp.float32
BF16 = jnp.bfloat16

D_MODEL = 1024
DEPTH = 2
W_GRP = 256
SC_WIDTH = 3
GLA_HEADS = 4
GLA_DK = 32
GLA_DV = 64
GLA_LOWRANK = 16
GLA_TAU = 16.0
MLA_HEADS = 4
MLA_V_DIM = 64
MLA_NOPE = 64
MLA_ROPE = 32
MLA_Q_RANK = 256
MLA_KV_RANK = 128
ROPE_BASE = 10000.0
CFM_WIDTH = 31
D_FF = 2816
FFN_CONV_WIDTH = 3
DN_ALPHA = (2.0 * DEPTH) ** 0.25
EPS = 1e-5

LANES = 128
SUBLANES = 8
VMEM_LIMIT_BYTES = 56 * 1024 * 1024

TOKEN_TILE = 512
HALO = 16
FFN_HALO = SUBLANES
FF_CHUNK = 256
GLA_CHUNK = 64
GLA_REF_ROW = GLA_CHUNK // 2
ATTN_TQ = 512
ATTN_TK = 512
HEAD_BLOCK = LANES
ROPE_LO = MLA_NOPE
ROPE_HALF = MLA_ROPE // 2
LOG2E = math.log2(math.e)

COL_SC_B, COL_SC_C, COL_SC_H = 0, 256, 512
COL_GQ, COL_GK, COL_GV, COL_GG = 768, 896, 1024, 1280
COL_MISC = 1536
COL_CQ, COL_CKV = 1664, 1920
COL_CFM_A, COL_CFM_G = 2048, 2304
D_IN_PAD = 2560


def _params(*sem):
    return pltpu.CompilerParams(dimension_semantics=sem, vmem_limit_bytes=VMEM_LIMIT_BYTES)


def _dot(a, b):
    return jnp.dot(a, b, preferred_element_type=F32)


def _dot_nt(a, b):
    return lax.dot_general(a, b, (((1,), (1,)), ((), ())), preferred_element_type=F32)


def _dot_tn(a, b):
    return lax.dot_general(a, b, (((0,), (0,)), ((), ())), preferred_element_type=F32)


def _layer_norm(x, g, b):
    mu = jnp.mean(x, axis=-1, keepdims=True)
    xc = x - mu
    var = jnp.mean(xc * xc, axis=-1, keepdims=True)
    return xc * lax.rsqrt(var + EPS) * g + b


def _rms_norm(x, g):
    return x * lax.rsqrt(jnp.mean(x * x, axis=-1, keepdims=True) + EPS) * g


def _sigmoid(x):
    return 1.0 / (1.0 + jnp.exp(-x))


def _log_sigmoid(x):
    return jnp.minimum(x, 0.0) - jnp.log1p(jnp.exp(-jnp.abs(x)))


def _const_spec(shape):
    return pl.BlockSpec(shape, lambda *_: (0,) * len(shape))


def _row_spec(rows, cols):
    return pl.BlockSpec((rows, cols), lambda i: (i, 0))


def _rope_kernel(pos_ref, inv_ref, c_ref, s1_ref, s2_ref):
    ang = pos_ref[...].astype(F32) * inv_ref[...]
    lane = lax.broadcasted_iota(jnp.int32, ang.shape, 1)
    first = (lane >= ROPE_LO) & (lane < ROPE_LO + ROPE_HALF)
    second = (lane >= ROPE_LO + ROPE_HALF) & (lane < ROPE_LO + MLA_ROPE)
    cos, sin = jnp.cos(ang), jnp.sin(ang)
    c_ref[...] = jnp.where(lane < ROPE_LO, 1.0, jnp.where(first | second, cos, 0.0))
    s1_ref[...] = jnp.where(second, sin, 0.0)
    s2_ref[...] = jnp.where(first, -sin, 0.0)


def _rope_tables(positions, tm):
    t = positions.size
    inv = ROPE_BASE ** (-jnp.arange(0, MLA_ROPE, 2, dtype=F32) / MLA_ROPE)
    inv_lane = jnp.zeros((1, LANES), F32)
    inv_lane = inv_lane.at[0, ROPE_LO:ROPE_LO + ROPE_HALF].set(inv)
    inv_lane = inv_lane.at[0, ROPE_LO + ROPE_HALF:ROPE_LO + MLA_ROPE].set(inv)
    out = jax.ShapeDtypeStruct((t, LANES), F32)
    return pl.pallas_call(
        _rope_kernel, grid=(t // tm,),
        in_specs=[_row_spec(tm, 1), _const_spec((1, LANES))],
        out_specs=[_row_spec(tm, LANES)] * 3, out_shape=[out] * 3,
        compiler_params=_params("parallel"), name="rope_tables",
    )(positions.reshape(t, 1), inv_lane)


def _rope(x, c, s1, s2):
    return (x * c + pltpu.roll(x, ROPE_HALF, 1) * s1
            + pltpu.roll(x, LANES - ROPE_HALF, 1) * s2)


def _ln_kernel(x_ref, g_ref, b_ref, o_ref):
    o_ref[...] = _layer_norm(x_ref[...], g_ref[...], b_ref[...])


def _input_norm(x, g, b, tm):
    t, d = x.shape
    return pl.pallas_call(
        _ln_kernel, grid=(t // tm,),
        in_specs=[_row_spec(tm, d), _const_spec((1, d)), _const_spec((1, d))],
        out_specs=_row_spec(tm, d), out_shape=jax.ShapeDtypeStruct((t, d), F32),
        compiler_params=_params("parallel"), name="input_norm",
    )(x, g.reshape(1, d), b.reshape(1, d))


def _inproj_kernel(h_ref, w_ref, wg_ref, bg_ref, gq_ref, wuq_ref, gkv_ref, wuk_ref, wuv_ref,
                   c_ref, s1_ref, s2_ref,
                   ab_ref, ap_ref, du_ref, q_ref, k_ref, v_ref, gg_ref, laf_ref, lab_ref,
                   qc_ref, kc_ref, vc_ref):
    hb = h_ref[...].astype(BF16)

    def proj(lo, n):
        return _dot(hb, w_ref[:, lo:lo + n])

    ab_ref[...] = proj(COL_SC_B, W_GRP)
    ap_ref[...] = proj(COL_SC_C, W_GRP) * proj(COL_SC_H, W_GRP)
    du_ref[...] = proj(COL_CFM_A, W_GRP) * _sigmoid(proj(COL_CFM_G, W_GRP))
    q_ref[...] = proj(COL_GQ, LANES) * (GLA_DK ** -0.5)
    k_ref[...] = proj(COL_GK, LANES)
    v_ref[...] = proj(COL_GV, W_GRP)
    gg_ref[...] = proj(COL_GG, W_GRP)
    misc = proj(COL_MISC, LANES)
    pre = _dot(misc.astype(BF16), wg_ref[...]) + bg_ref[...]
    la = _log_sigmoid(pre) * (1.0 / GLA_TAU)
    laf_ref[...] = la[:, :LANES]
    lab_ref[...] = la[:, LANES:]
    c, s1, s2 = c_ref[...], s1_ref[...], s2_ref[...]
    cqn = _rms_norm(proj(COL_CQ, MLA_Q_RANK), gq_ref[...]).astype(BF16)
    qm = _dot(cqn, wuq_ref[...])
    qscale = (MLA_NOPE + MLA_ROPE) ** -0.5 * LOG2E
    ckvn = _rms_norm(proj(COL_CKV, MLA_KV_RANK), gkv_ref[...]).astype(BF16)
    kn = _dot(ckvn, wuk_ref[...])
    vn = _dot(ckvn, wuv_ref[...])
    lane = lax.broadcasted_iota(jnp.int32, misc.shape, 1)
    kr = _rope(jnp.where(lane >= ROPE_LO, misc, 0.0), c, s1, s2)
    ones_col = jnp.where(lane == MLA_V_DIM, 1.0, 0.0)
    for hh in range(MLA_HEADS):
        sl = slice(hh * HEAD_BLOCK, (hh + 1) * HEAD_BLOCK)
        qc_ref[:, sl] = (_rope(qm[:, sl], c, s1, s2) * qscale).astype(BF16)
        kc_ref[:, sl] = (kn[:, sl] + kr).astype(BF16)
        vc_ref[:, sl] = (vn[:, sl] + ones_col).astype(BF16)


def _inproj(h, lw, rope, tm):
    t = h.shape[0]
    f = lambda n, dt=F32: jax.ShapeDtypeStruct((t, n), dt)
    hb4 = MLA_HEADS * HEAD_BLOCK
    out_shape = [f(W_GRP), f(W_GRP), f(W_GRP), f(LANES), f(LANES), f(W_GRP), f(W_GRP), f(LANES), f(LANES),
                 f(hb4, BF16), f(hb4, BF16), f(hb4, BF16)]
    out_specs = [_row_spec(tm, s.shape[1]) for s in out_shape]
    in_specs = [_row_spec(tm, D_MODEL), _const_spec((D_MODEL, D_IN_PAD)),
                _const_spec((LANES, 2 * LANES)), _const_spec((1, 2 * LANES)),
                _const_spec((1, MLA_Q_RANK)), _const_spec((MLA_Q_RANK, hb4)),
                _const_spec((1, MLA_KV_RANK)), _const_spec((MLA_KV_RANK, hb4)), _const_spec((MLA_KV_RANK, hb4)),
                _row_spec(tm, LANES), _row_spec(tm, LANES), _row_spec(tm, LANES)]
    return pl.pallas_call(
        _inproj_kernel, grid=(t // tm,), in_specs=in_specs, out_specs=out_specs, out_shape=out_shape,
        compiler_params=_params("parallel"), name="inproj",
    )(h, lw["w_in"], lw["w_gate"], lw["b_gate"], lw["g_mla_q"], lw["w_uq"], lw["g_mla_kv"], lw["w_uk"],
      lw["w_uv"], *rope)


def _gla_kernel(q_ref, k_ref, v_ref, laf_ref, lab_ref, gg_ref, ghead_ref, gain_ref, out_ref,
                acc_ref, sf_ref, sb_ref, *, seq):
    cs = GLA_CHUNK
    n_chunks = seq // cs
    hk = GLA_HEADS * GLA_DK
    hv = GLA_HEADS * GLA_DV
    row = lax.broadcasted_iota(jnp.int32, (cs, cs), 0)
    col = lax.broadcasted_iota(jnp.int32, (cs, cs), 1)
    tri = jnp.where(row >= col, 1.0, 0.0).astype(BF16)
    srow = lax.broadcasted_iota(jnp.int32, (GLA_HEADS * cs, cs), 0) % cs
    scol = lax.broadcasted_iota(jnp.int32, (GLA_HEADS * cs, cs), 1)
    keep = (srow >= scol, srow < scol)
    qlane_head = lax.broadcasted_iota(jnp.int32, (cs, hk), 1) // GLA_DK
    olane_head = lax.broadcasted_iota(jnp.int32, (cs, hv), 1) // GLA_DV
    srow_head = lax.broadcasted_iota(jnp.int32, (hk, hv), 0) // GLA_DK
    scol_head = lax.broadcasted_iota(jnp.int32, (hk, hv), 1) // GLA_DV
    eye = (lax.broadcasted_iota(jnp.int32, (hk, hk), 0) == lax.broadcasted_iota(jnp.int32, (hk, hk), 1))

    acc_ref[...] = jnp.zeros_like(acc_ref)
    sf_ref[...] = jnp.zeros_like(sf_ref)
    sb_ref[...] = jnp.zeros_like(sb_ref)

    def chunk_step(c, direction):
        la_ref, s_ref = ((laf_ref, sf_ref), (lab_ref, sb_ref))[direction]
        rows = pl.ds(pl.multiple_of(c * cs, cs), cs)
        la = la_ref[rows, :]
        hi = la.astype(BF16)
        rem = la - hi.astype(F32)
        mid = rem.astype(BF16)
        lo = (rem - mid.astype(F32)).astype(BF16)
        cum = _dot(tri, hi) + _dot(tri, mid) + _dot(tri, lo)
        total = cum[cs - 1:cs, :]
        b = cum if direction == 0 else total - cum + la
        ref = b[GLA_REF_ROW:GLA_REF_ROW + 1, :]
        qt = q_ref[rows, :] * jnp.exp(b - ref)
        kt = k_ref[rows, :] * jnp.exp(ref - b)
        vb = v_ref[rows, :].astype(BF16)
        qbd = jnp.concatenate([jnp.where(qlane_head == hh, qt, 0.0) for hh in range(GLA_HEADS)], axis=0)
        scores = _dot_nt(qbd.astype(BF16), kt.astype(BF16))
        p = jnp.where(keep[direction], scores, 0.0).astype(BF16)
        state = s_ref[...]
        qin = (qbd * jnp.exp(ref)).astype(BF16)
        o_all = _dot(p, vb) + _dot(qin, state.astype(BF16))
        o = jnp.zeros((cs, hv), F32)
        for hh in range(GLA_HEADS):
            o = o + jnp.where(olane_head == hh, o_all[hh * cs:(hh + 1) * cs, :], 0.0)
        acc_ref[rows, :] += o
        khat = (kt * jnp.exp(total - ref)).astype(BF16)
        kv = jnp.where(srow_head == scol_head, _dot_tn(khat, vb), 0.0)
        decay_col = jnp.sum(jnp.where(eye, jnp.broadcast_to(jnp.exp(total), (hk, hk)), 0.0),
                            axis=1, keepdims=True)
        s_ref[...] = state * decay_col + kv

    def body(n, carry):
        chunk_step(n, 0)
        chunk_step(n_chunks - 1 - n, 1)
        return carry

    lax.fori_loop(0, n_chunks, body, 0)

    hrow = lax.broadcasted_iota(jnp.int32, (hv, hv), 0) // GLA_DV
    hcol = lax.broadcasted_iota(jnp.int32, (hv, hv), 1) // GLA_DV
    head_ones = jnp.where(hrow == hcol, 1.0, 0.0).astype(BF16)
    ft = min(TOKEN_TILE, seq)

    def finish(i, carry):
        rows = pl.ds(pl.multiple_of(i * ft, ft), ft)
        o = acc_ref[rows, :]
        sq = o * o
        hi = sq.astype(BF16)
        lo = (sq - hi.astype(F32)).astype(BF16)
        ms = (_dot(hi, head_ones) + _dot(lo, head_ones)) * (1.0 / GLA_DV)
        on = o * lax.rsqrt(ms + EPS) * ghead_ref[...]
        gg = gg_ref[rows, :]
        yb = on * (gg * _sigmoid(gg))
        out_ref[rows, :] = _rms_norm(yb, gain_ref[...]).astype(BF16)
        return carry

    lax.fori_loop(0, seq // ft, finish, 0)


def _gla(q, k, v, laf, lab, gg, ghead, gain, bsz, seq):
    hk, hv = GLA_HEADS * GLA_DK, GLA_HEADS * GLA_DV
    seq_spec = lambda n: pl.BlockSpec((seq, n), lambda b: (b, 0))
    return pl.pallas_call(
        functools.partial(_gla_kernel, seq=seq), grid=(bsz,),
        in_specs=[seq_spec(hk), seq_spec(hk), seq_spec(hv), seq_spec(hk), seq_spec(hk), seq_spec(hv),
                  _const_spec((1, hv)), _const_spec((1, hv))],
        out_specs=seq_spec(hv), out_shape=jax.ShapeDtypeStruct((bsz * seq, hv), BF16),
        scratch_shapes=[pltpu.VMEM((seq, hv), F32), pltpu.VMEM((hk, hv), F32), pltpu.VMEM((hk, hv), F32)],
        compiler_params=_params("parallel"), name="gla",
    )(q, k, v, laf, lab, gg, ghead, gain)


def _attn_kernel(q_ref, k_ref, v_ref, gain_ref, out_ref, *, seq, tk):
    tq = q_ref.shape[0]
    heads = []
    for hh in range(MLA_HEADS):
        sl = slice(hh * HEAD_BLOCK, (hh + 1) * HEAD_BLOCK)
        qh = q_ref[:, sl]

        def body(j, carry, sl=sl, qh=qh):
            m, acc = carry
            rows = pl.ds(pl.multiple_of(j * tk, tk), tk)
            s = _dot_nt(qh, k_ref[rows, sl])
            m_new = jnp.maximum(m, jnp.max(s, axis=1, keepdims=True))
            p = jnp.exp2(s - m_new)
            acc = acc * jnp.exp2(m - m_new) + _dot(p.astype(BF16), v_ref[rows, sl])
            return m_new, acc

        m0 = jnp.full((tq, 1), -jnp.inf, F32)
        acc0 = jnp.zeros((tq, HEAD_BLOCK), F32)
        _, acc = lax.fori_loop(0, seq // tk, body, (m0, acc0))
        heads.append(acc[:, :MLA_V_DIM] / acc[:, MLA_V_DIM:MLA_V_DIM + 1])
    y = jnp.concatenate(heads, axis=1)
    out_ref[...] = _rms_norm(y, gain_ref[...]).astype(BF16)


def _attention(qc, kc, vc, gain, bsz, seq):
    tq, tk = min(ATTN_TQ, seq), min(ATTN_TK, seq)
    nq = seq // tq
    hb4 = MLA_HEADS * HEAD_BLOCK
    kv_spec = pl.BlockSpec((seq, hb4), lambda b, i: (b, 0))
    return pl.pallas_call(
        functools.partial(_attn_kernel, seq=seq, tk=tk), grid=(bsz, nq),
        in_specs=[pl.BlockSpec((tq, hb4), lambda b, i: (b * nq + i, 0)), kv_spec, kv_spec,
                  _const_spec((1, W_GRP))],
        out_specs=pl.BlockSpec((tq, W_GRP), lambda b, i: (b * nq + i, 0)),
        out_shape=jax.ShapeDtypeStruct((bsz * seq, W_GRP), BF16),
        compiler_params=_params("parallel", "parallel"), name="mla_attention",
    )(qc, kc, vc, gain)


def _outproj_kernel(ab_ref, ap_ref, app_ref, apn_ref, du_ref, dup_ref, dun_ref, yb_ref, yc_ref, h_ref,
                    wsc_ref, wcfm_ref, gcfm_ref, bcfm_ref, gain_ref, wout_ref, g1_ref, b1_ref,
                    out_ref, pext_ref, uext_ref, *, tiles_per_seq):
    tm = ab_ref.shape[0]
    i = pl.program_id(0) % tiles_per_seq
    not_first = (i != 0).astype(F32)
    not_last = (i != tiles_per_seq - 1).astype(F32)
    pext_ref[0:HALO, :] = app_ref[...] * not_first
    pext_ref[HALO:HALO + tm, :] = ap_ref[...]
    pext_ref[HALO + tm:, :] = apn_ref[...] * not_last
    uext_ref[0:HALO, :] = dup_ref[...] * not_first
    uext_ref[HALO:HALO + tm, :] = du_ref[...]
    uext_ref[HALO + tm:, :] = dun_ref[...] * not_last

    conv = jnp.zeros((tm, W_GRP), F32)
    for j in range(SC_WIDTH):
        conv = conv + pext_ref[pl.ds(HALO - SC_WIDTH // 2 + j, tm), :] * wsc_ref[j:j + 1, :]
    ya = _rms_norm(ab_ref[...] * conv, gain_ref[:, 0:W_GRP])
    u = jnp.zeros((tm, W_GRP), F32)
    for j in range(CFM_WIDTH):
        u = u + uext_ref[pl.ds(HALO - CFM_WIDTH // 2 + j, tm), :] * wcfm_ref[j:j + 1, :]
    un = _layer_norm(u, gcfm_ref[...], bcfm_ref[...])
    yd = _rms_norm(un * _sigmoid(un), gain_ref[:, 3 * W_GRP:4 * W_GRP])

    mix = (_dot(ya.astype(BF16), wout_ref[0:W_GRP, :])
           + _dot(yb_ref[...], wout_ref[W_GRP:2 * W_GRP, :])
           + _dot(yc_ref[...], wout_ref[2 * W_GRP:3 * W_GRP, :])
           + _dot(yd.astype(BF16), wout_ref[3 * W_GRP:4 * W_GRP, :]))
    out_ref[...] = _layer_norm(DN_ALPHA * h_ref[...] + mix, g1_ref[...], b1_ref[...])


def _halo_specs(tm, cols, halo, total_rows):
    per = tm // halo
    last = total_rows // halo - 1
    prev = pl.BlockSpec((halo, cols), lambda i: (jnp.maximum(i * per - 1, 0), 0))
    nxt = pl.BlockSpec((halo, cols), lambda i: (jnp.minimum((i + 1) * per, last), 0))
    return prev, nxt


def _outproj(ab, ap, du, yb, yc, h, lw, seq, tm):
    t = h.shape[0]
    prev, nxt = _halo_specs(tm, W_GRP, HALO, t)
    g = _row_spec(tm, W_GRP)
    in_specs = [g, g, prev, nxt, g, prev, nxt, g, g, _row_spec(tm, D_MODEL),
                _const_spec((SC_WIDTH, W_GRP)), _const_spec((CFM_WIDTH, W_GRP)),
                _const_spec((1, W_GRP)), _const_spec((1, W_GRP)), _const_spec((1, 4 * W_GRP)),
                _const_spec((4 * W_GRP, D_MODEL)), _const_spec((1, D_MODEL)), _const_spec((1, D_MODEL))]
    return pl.pallas_call(
        functools.partial(_outproj_kernel, tiles_per_seq=seq // tm), grid=(t // tm,),
        in_specs=in_specs, out_specs=_row_spec(tm, D_MODEL),
        out_shape=jax.ShapeDtypeStruct((t, D_MODEL), F32),
        scratch_shapes=[pltpu.VMEM((tm + 2 * HALO, W_GRP), F32), pltpu.VMEM((tm + 2 * HALO, W_GRP), F32)],
        compiler_params=_params("parallel"), name="outproj",
    )(ab, ap, ap, ap, du, du, du, yb, yc, h, lw["w_sc_conv"], lw["w_cfm_dw"], lw["g_cfm_ln"],
      lw["b_cfm_ln"], lw["g_branch"], lw["w_out"], lw["ln1_g"], lw["ln1_b"])


def _ffn_kernel(h_ref, hp_ref, hn_ref, wup_ref, wdw_ref, wdown_ref, g2_ref, b2_ref, out_ref,
                hx_ref, acc_ref, *, tiles_per_seq, n_chunks):
    tm = h_ref.shape[0]
    i = pl.program_id(0) % tiles_per_seq
    not_first = (i != 0).astype(F32)
    not_last = (i != tiles_per_seq - 1).astype(F32)
    ext = tm + 2 * FFN_HALO
    hx_ref[0:FFN_HALO, :] = (hp_ref[...] * not_first).astype(BF16)
    hx_ref[FFN_HALO:FFN_HALO + tm, :] = h_ref[...].astype(BF16)
    hx_ref[FFN_HALO + tm:, :] = (hn_ref[...] * not_last).astype(BF16)
    acc_ref[...] = jnp.zeros_like(acc_ref)

    def conv3(u, w):
        prev = pltpu.roll(u, 1, 0)
        nxt = pltpu.roll(u, ext - 1, 0)
        full = prev * w[0:1, :] + u * w[1:2, :] + nxt * w[2:3, :]
        return full[FFN_HALO:FFN_HALO + tm, :]

    def body(c, carry):
        hx = hx_ref[...]
        gate = conv3(_dot(hx, wup_ref[0, c]), wdw_ref[0, c])
        val = conv3(_dot(hx, wup_ref[1, c]), wdw_ref[1, c])
        act = (gate * _sigmoid(gate) * val).astype(BF16)
        acc_ref[...] += _dot(act, wdown_ref[c])
        return carry

    lax.fori_loop(0, n_chunks, body, 0)
    out_ref[...] = _layer_norm(DN_ALPHA * h_ref[...] + acc_ref[...], g2_ref[...], b2_ref[...])


def _ffn(h, lw, seq, tm):
    t = h.shape[0]
    n_chunks = D_FF // FF_CHUNK
    prev, nxt = _halo_specs(tm, D_MODEL, FFN_HALO, t)
    in_specs = [_row_spec(tm, D_MODEL), prev, nxt,
                _const_spec((2, n_chunks, D_MODEL, FF_CHUNK)), _const_spec((2, n_chunks, FFN_CONV_WIDTH, FF_CHUNK)),
                _const_spec((n_chunks, FF_CHUNK, D_MODEL)), _const_spec((1, D_MODEL)), _const_spec((1, D_MODEL))]
    return pl.pallas_call(
        functools.partial(_ffn_kernel, tiles_per_seq=seq // tm, n_chunks=n_chunks), grid=(t // tm,),
        in_specs=in_specs, out_specs=_row_spec(tm, D_MODEL),
        out_shape=jax.ShapeDtypeStruct((t, D_MODEL), F32),
        scratch_shapes=[pltpu.VMEM((tm + 2 * FFN_HALO, D_MODEL), BF16), pltpu.VMEM((tm, D_MODEL), F32)],
        compiler_params=_params("parallel"), name="conv_ffn",
    )(h, h, h, lw["w_ffn_up"], lw["w_ffn_dw"], lw["w_ffn_down"], lw["ln2_g"], lw["ln2_b"])


def _head_blocks(w, per_head, lo, n):
    rows = w.shape[0]
    wh = w.reshape(rows, -1, per_head)[:, :, lo:lo + n]
    return jnp.pad(wh, ((0, 0), (0, 0), (0, HEAD_BLOCK - n))).reshape(rows, -1)


def _layer_weights(p, l):
    w_in = p["w_in"][l]
    sizes = (256, 256, 256, 128, 128, 256, 256, 16, 16, 256, 128, 32, 512)
    offs = [0]
    for s in sizes:
        offs.append(offs[-1] + s)
    seg = lambda idx: w_in[:, offs[idx]:offs[idx + 1]]
    zeros = lambda n: jnp.zeros((D_MODEL, n), F32)
    misc = jnp.concatenate([seg(7), seg(8), zeros(ROPE_LO - 2 * GLA_LOWRANK), seg(11),
                            zeros(LANES - ROPE_LO - MLA_ROPE)], axis=1)
    w_in_p = jnp.concatenate([seg(0), seg(1), seg(2), seg(3), seg(4), seg(5), seg(6), misc, seg(9), seg(10),
                              seg(12)], axis=1).astype(BF16)
    hk = GLA_HEADS * GLA_DK
    w_gate = jnp.zeros((LANES, 2 * hk), F32)
    w_gate = w_gate.at[0:GLA_LOWRANK, 0:hk].set(p["w_gla_a_up"][l, 0])
    w_gate = w_gate.at[GLA_LOWRANK:2 * GLA_LOWRANK, hk:].set(p["w_gla_a_up"][l, 1])
    w_uq = p["w_mla_uq"][l]
    w_ukv = p["w_mla_ukv"][l]
    n_ff = D_FF // FF_CHUNK
    w_up = p["w_ffn_up"][l].reshape(D_MODEL, 2, n_ff, FF_CHUNK).transpose(1, 2, 0, 3)
    w_dw = p["w_ffn_dw"][l].reshape(FFN_CONV_WIDTH, 2, n_ff, FF_CHUNK).transpose(1, 2, 0, 3)
    row = lambda a: a.reshape(1, -1)
    return {
        "w_in": w_in_p,
        "w_gate": w_gate.astype(BF16),
        "b_gate": row(p["b_gla_a"][l]),
        "g_mla_q": row(p["g_mla_q"][l]),
        "w_uq": _head_blocks(w_uq, MLA_NOPE + MLA_ROPE, 0, MLA_NOPE + MLA_ROPE).astype(BF16),
        "g_mla_kv": row(p["g_mla_kv"][l]),
        "w_uk": _head_blocks(w_ukv, MLA_NOPE + MLA_V_DIM, 0, MLA_NOPE).astype(BF16),
        "w_uv": _head_blocks(w_ukv, MLA_NOPE + MLA_V_DIM, MLA_NOPE, MLA_V_DIM).astype(BF16),
        "g_gla_head": row(p["g_gla_head"][l]),
        "w_sc_conv": p["w_sc_conv"][l],
        "w_cfm_dw": p["w_cfm_dw"][l],
        "g_cfm_ln": row(p["g_cfm_ln"][l]),
        "b_cfm_ln": row(p["b_cfm_ln"][l]),
        "g_branch": row(p["g_branch"][l]),
        "w_out": p["w_out"][l].astype(BF16),
        "ln1_g": row(p["ln1_g"][l]),
        "ln1_b": row(p["ln1_b"][l]),
        "w_ffn_up": w_up.astype(BF16),
        "w_ffn_dw": w_dw,
        "w_ffn_down": p["w_ffn_down"][l].reshape(n_ff, FF_CHUNK, D_MODEL).astype(BF16),
        "ln2_g": row(p["ln2_g"][l]),
        "ln2_b": row(p["ln2_b"][l]),
    }


def kernel(x, positions, ln_in_g, ln_in_b, w_in, w_sc_conv, w_gla_a_up, b_gla_a, g_gla_head, g_mla_q, w_mla_uq,
           g_mla_kv, w_mla_ukv, w_cfm_dw, g_cfm_ln, b_cfm_ln, g_branch, w_out, ln1_g, ln1_b, w_ffn_up, w_ffn_dw,
           w_ffn_down, ln2_g, ln2_b):
    p = dict(w_in=w_in, w_sc_conv=w_sc_conv, w_gla_a_up=w_gla_a_up, b_gla_a=b_gla_a, g_gla_head=g_gla_head,
             g_mla_q=g_mla_q, w_mla_uq=w_mla_uq, g_mla_kv=g_mla_kv, w_mla_ukv=w_mla_ukv, w_cfm_dw=w_cfm_dw,
             g_cfm_ln=g_cfm_ln, b_cfm_ln=b_cfm_ln, g_branch=g_branch, w_out=w_out, ln1_g=ln1_g, ln1_b=ln1_b,
             w_ffn_up=w_ffn_up, w_ffn_dw=w_ffn_dw, w_ffn_down=w_ffn_down, ln2_g=ln2_g, ln2_b=ln2_b)
    bsz, seq, d = x.shape
    assert d == D_MODEL and seq % GLA_CHUNK == 0
    tm = min(TOKEN_TILE, seq)
    assert seq % tm == 0 and tm % HALO == 0
    t = bsz * seq
    rope = _rope_tables(positions, tm)
    h = _input_norm(x.reshape(t, d), ln_in_g, ln_in_b, tm)
    for l in range(DEPTH):
        lw = _layer_weights(p, l)
        ab, ap, du, q, k, v, gg, laf, lab, qc, kc, vc = _inproj(h, lw, rope, tm)
        gain = lw["g_branch"]
        yb = _gla(q, k, v, laf, lab, gg, lw["g_gla_head"], gain[:, W_GRP:2 * W_GRP], bsz, seq)
        yc = _attention(qc, kc, vc, gain[:, 2 * W_GRP:3 * W_GRP], bsz, seq)
        h = _outproj(ab, ap, du, yb, yc, h, lw, seq, tm)
        h = _ffn(h, lw, seq, tm)
    return h.reshape(bsz, seq, d)
```

```python
import functools
import math

import jax
import jax.numpy as jnp
from jax import lax
from jax.experimental import pallas as pl
from jax.experimental.pallas import tpu as pltpu

F32 = jnp.float32
BF16 = jnp.bfloat16

D_MODEL = 1024
DEPTH = 2
W_GRP = 256
SC_WIDTH = 3
GLA_HEADS = 4
GLA_DK = 32
GLA_DV = 64
GLA_LOWRANK = 16
GLA_TAU = 16.0
MLA_HEADS = 4
MLA_V_DIM = 64
MLA_NOPE = 64
MLA_ROPE = 32
MLA_Q_RANK = 256
MLA_KV_RANK = 128
ROPE_BASE = 10000.0
CFM_WIDTH = 31
D_FF = 2816
FFN_CONV_WIDTH = 3
DN_ALPHA = (2.0 * DEPTH) ** 0.25
EPS = 1e-5

LANES = 128
SUBLANES = 8
VMEM_LIMIT_BYTES = 56 * 1024 * 1024

TOKEN_TILE = 512
HALO = 16
FFN_HALO = SUBLANES
FF_CHUNK = 256
GLA_CHUNK = 64
GLA_REF_ROW = GLA_CHUNK // 2
ATTN_TQ = 512
ATTN_TK = 512
HEAD_BLOCK = LANES
ROPE_LO = MLA_NOPE
ROPE_HALF = MLA_ROPE // 2
LOG2E = math.log2(math.e)

COL_SC_B, COL_SC_C, COL_SC_H = 0, 256, 512
COL_GQ, COL_GK, COL_GV, COL_GG = 768, 896, 1024, 1280
COL_MISC = 1536
COL_CQ, COL_CKV = 1664, 1920
COL_CFM_A, COL_CFM_G = 2048, 2304
D_IN_PAD = 2560


def _params(*sem):
    return pltpu.CompilerParams(dimension_semantics=sem, vmem_limit_bytes=VMEM_LIMIT_BYTES)


def _dot(a, b):
    return jnp.dot(a, b, preferred_element_type=F32)


def _dot_nt(a, b):
    return lax.dot_general(a, b, (((1,), (1,)), ((), ())), preferred_element_type=F32)


def _dot_tn(a, b):
    return lax.dot_general(a, b, (((0,), (0,)), ((), ())), preferred_element_type=F32)


def _layer_norm(x, g, b):
    mu = jnp.mean(x, axis=-1, keepdims=True)
    xc = x - mu
    var = jnp.mean(xc * xc, axis=-1, keepdims=True)
    return xc * lax.rsqrt(var + EPS) * g + b


def _rms_norm(x, g):
    return x * lax.rsqrt(jnp.mean(x * x, axis=-1, keepdims=True) + EPS) * g


def _sigmoid(x):
    return 1.0 / (1.0 + jnp.exp(-x))


def _log_sigmoid(x):
    return jnp.minimum(x, 0.0) - jnp.log1p(jnp.exp(-jnp.abs(x)))


def _const_spec(shape):
    return pl.BlockSpec(shape, lambda *_: (0,) * len(shape))


def _row_spec(rows, cols):
    return pl.BlockSpec((rows, cols), lambda i: (i, 0))


def _rope_kernel(pos_ref, inv_ref, c_ref, s1_ref, s2_ref):
    ang = pos_ref[...].astype(F32) * inv_ref[...]
    lane = lax.broadcasted_iota(jnp.int32, ang.shape, 1)
    first = (lane >= ROPE_LO) & (lane < ROPE_LO + ROPE_HALF)
    second = (lane >= ROPE_LO + ROPE_HALF) & (lane < ROPE_LO + MLA_ROPE)
    cos, sin = jnp.cos(ang), jnp.sin(ang)
    c_ref[...] = jnp.where(lane < ROPE_LO, 1.0, jnp.where(first | second, cos, 0.0))
    s1_ref[...] = jnp.where(second, sin, 0.0)
    s2_ref[...] = jnp.where(first, -sin, 0.0)


def _rope_tables(positions, tm):
    t = positions.size
    inv = ROPE_BASE ** (-jnp.arange(0, MLA_ROPE, 2, dtype=F32) / MLA_ROPE)
    inv_lane = jnp.zeros((1, LANES), F32)
    inv_lane = inv_lane.at[0, ROPE_LO:ROPE_LO + ROPE_HALF].set(inv)
    inv_lane = inv_lane.at[0, ROPE_LO + ROPE_HALF:ROPE_LO + MLA_ROPE].set(inv)
    out = jax.ShapeDtypeStruct((t, LANES), F32)
    return pl.pallas_call(
        _rope_kernel, grid=(t // tm,),
        in_specs=[_row_spec(tm, 1), _const_spec((1, LANES))],
        out_specs=[_row_spec(tm, LANES)] * 3, out_shape=[out] * 3,
        compiler_params=_params("parallel"), name="rope_tables",
    )(positions.reshape(t, 1), inv_lane)


def _rope(x, c, s1, s2):
    return (x * c + pltpu.roll(x, ROPE_HALF, 1) * s1
            + pltpu.roll(x, LANES - ROPE_HALF, 1) * s2)


def _ln_kernel(x_ref, g_ref, b_ref, o_ref):
    o_ref[...] = _layer_norm(x_ref[...], g_ref[...], b_ref[...])


def _input_norm(x, g, b, tm):
    t, d = x.shape
    return pl.pallas_call(
        _ln_kernel, grid=(t // tm,),
        in_specs=[_row_spec(tm, d), _const_spec((1, d)), _const_spec((1, d))],
        out_specs=_row_spec(tm, d), out_shape=jax.ShapeDtypeStruct((t, d), F32),
        compiler_params=_params("parallel"), name="input_norm",
    )(x, g.reshape(1, d), b.reshape(1, d))


def _inproj_kernel(h_ref, w_ref, wg_ref, bg_ref, gq_ref, wuq_ref, gkv_ref, wuk_ref, wuv_ref,
                   c_ref, s1_ref, s2_ref,
                   ab_ref, ap_ref, du_ref, q_ref, k_ref, v_ref, gg_ref, laf_ref, lab_ref,
                   qc_ref, kc_ref, vc_ref):
    hb = h_ref[...].astype(BF16)

    def proj(lo, n):
        return _dot(hb, w_ref[:, lo:lo + n])

    ab_ref[...] = proj(COL_SC_B, W_GRP)
    ap_ref[...] = proj(COL_SC_C, W_GRP) * proj(COL_SC_H, W_GRP)
    du_ref[...] = proj(COL_CFM_A, W_GRP) * _sigmoid(proj(COL_CFM_G, W_GRP))
    q_ref[...] = proj(COL_GQ, LANES) * (GLA_DK ** -0.5)
    k_ref[...] = proj(COL_GK, LANES)
    v_ref[...] = proj(COL_GV, W_GRP)
    gg_ref[...] = proj(COL_GG, W_GRP)
    misc = proj(COL_MISC, LANES)
    pre = _dot(misc.astype(BF16), wg_ref[...]) + bg_ref[...]
    la = _log_sigmoid(pre) * (1.0 / GLA_TAU)
    laf_ref[...] = la[:, :LANES]
    lab_ref[...] = la[:, LANES:]
    c, s1, s2 = c_ref[...], s1_ref[...], s2_ref[...]
    cqn = _rms_norm(proj(COL_CQ, MLA_Q_RANK), gq_ref[...]).astype(BF16)
    qm = _dot(cqn, wuq_ref[...])
    qscale = (MLA_NOPE + MLA_ROPE) ** -0.5 * LOG2E
    ckvn = _rms_norm(proj(COL_CKV, MLA_KV_RANK), gkv_ref[...]).astype(BF16)
    kn = _dot(ckvn, wuk_ref[...])
    vn = _dot(ckvn, wuv_ref[...])
    lane = lax.broadcasted_iota(jnp.int32, misc.shape, 1)
    kr = _rope(jnp.where(lane >= ROPE_LO, misc, 0.0), c, s1, s2)
    ones_col = jnp.where(lane == MLA_V_DIM, 1.0, 0.0)
    for hh in range(MLA_HEADS):
        sl = slice(hh * HEAD_BLOCK, (hh + 1) * HEAD_BLOCK)
        qc_ref[:, sl] = (_rope(qm[:, sl], c, s1, s2) * qscale).astype(BF16)
        kc_ref[:, sl] = (kn[:, sl] + kr).astype(BF16)
        vc_ref[:, sl] = (vn[:, sl] + ones_col).astype(BF16)


def _inproj(h, lw, rope, tm):
    t = h.shape[0]
    f = lambda n, dt=F32: jax.ShapeDtypeStruct((t, n), dt)
    hb4 = MLA_HEADS * HEAD_BLOCK
    out_shape = [f(W_GRP), f(W_GRP), f(W_GRP), f(LANES), f(LANES), f(W_GRP), f(W_GRP), f(LANES), f(LANES),
                 f(hb4, BF16), f(hb4, BF16), f(hb4, BF16)]
    out_specs = [_row_spec(tm, s.shape[1]) for s in out_shape]
    in_specs = [_row_spec(tm, D_MODEL), _const_spec((D_MODEL, D_IN_PAD)),
                _const_spec((LANES, 2 * LANES)), _const_spec((1, 2 * LANES)),
                _const_spec((1, MLA_Q_RANK)), _const_spec((MLA_Q_RANK, hb4)),
                _const_spec((1, MLA_KV_RANK)), _const_spec((MLA_KV_RANK, hb4)), _const_spec((MLA_KV_RANK, hb4)),
                _row_spec(tm, LANES), _row_spec(tm, LANES), _row_spec(tm, LANES)]
    return pl.pallas_call(
        _inproj_kernel, grid=(t // tm,), in_specs=in_specs, out_specs=out_specs, out_shape=out_shape,
        compiler_params=_params("parallel"), name="inproj",
    )(h, lw["w_in"], lw["w_gate"], lw["b_gate"], lw["g_mla_q"], lw["w_uq"], lw["g_mla_kv"], lw["w_uk"],
      lw["w_uv"], *rope)


def _gla_kernel(q_ref, k_ref, v_ref, laf_ref, lab_ref, gg_ref, ghead_ref, gain_ref, out_ref,
                acc_ref, sf_ref, sb_ref, *, seq):
    cs = GLA_CHUNK
    n_chunks = seq // cs
    hk = GLA_HEADS * GLA_DK
    hv = GLA_HEADS * GLA_DV
    row = lax.broadcasted_iota(jnp.int32, (cs, cs), 0)
    col = lax.broadcasted_iota(jnp.int32, (cs, cs), 1)
    tri = jnp.where(row >= col, 1.0, 0.0).astype(BF16)
    srow = lax.broadcasted_iota(jnp.int32, (GLA_HEADS * cs, cs), 0) % cs
    scol = lax.broadcasted_iota(jnp.int32, (GLA_HEADS * cs, cs), 1)
    keep = (srow >= scol, srow < scol)
    qlane_head = lax.broadcasted_iota(jnp.int32, (cs, hk), 1) // GLA_DK
    olane_head = lax.broadcasted_iota(jnp.int32, (cs, hv), 1) // GLA_DV
    srow_head = lax.broadcasted_iota(jnp.int32, (hk, hv), 0) // GLA_DK
    scol_head = lax.broadcasted_iota(jnp.int32, (hk, hv), 1) // GLA_DV
    eye = (lax.broadcasted_iota(jnp.int32, (hk, hk), 0) == lax.broadcasted_iota(jnp.int32, (hk, hk), 1))

    acc_ref[...] = jnp.zeros_like(acc_ref)
    sf_ref[...] = jnp.zeros_like(sf_ref)
    sb_ref[...] = jnp.zeros_like(sb_ref)

    def chunk_step(c, direction):
        la_ref, s_ref = ((laf_ref, sf_ref), (lab_ref, sb_ref))[direction]
        rows = pl.ds(pl.multiple_of(c * cs, cs), cs)
        la = la_ref[rows, :]
        hi = la.astype(BF16)
        rem = la - hi.astype(F32)
        mid = rem.astype(BF16)
        lo = (rem - mid.astype(F32)).astype(BF16)
        cum = _dot(tri, hi) + _dot(tri, mid) + _dot(tri, lo)
        total = cum[cs - 1:cs, :]
        b = cum if direction == 0 else total - cum + la
        ref = b[GLA_REF_ROW:GLA_REF_ROW + 1, :]
        qt = q_ref[rows, :] * jnp.exp(b - ref)
        kt = k_ref[rows, :] * jnp.exp(ref - b)
        vb = v_ref[rows, :].astype(BF16)
        qbd = jnp.concatenate([jnp.where(qlane_head == hh, qt, 0.0) for hh in range(GLA_HEADS)], axis=0)
        scores = _dot_nt(qbd.astype(BF16), kt.astype(BF16))
        p = jnp.where(keep[direction], scores, 0.0).astype(BF16)
        state = s_ref[...]
        qin = (qbd * jnp.exp(ref)).astype(BF16)
        o_all = _dot(p, vb) + _dot(qin, state.astype(BF16))
        o = jnp.zeros((cs, hv), F32)
        for hh in range(GLA_HEADS):
            o = o + jnp.where(olane_head == hh, o_all[hh * cs:(hh + 1) * cs, :], 0.0)
        acc_ref[rows, :] += o
        khat = (kt * jnp.exp(total - ref)).astype(BF16)
        kv = jnp.where(srow_head == scol_head, _dot_tn(khat, vb), 0.0)
        decay_col = jnp.sum(jnp.where(eye, jnp.broadcast_to(jnp.exp(total), (hk, hk)), 0.0),
                            axis=1, keepdims=True)
        s_ref[...] = state * decay_col + kv

    def body(n, carry):
        chunk_step(n, 0)
        chunk_step(n_chunks - 1 - n, 1)
        return carry

    lax.fori_loop(0, n_chunks, body, 0)

    hrow = lax.broadcasted_iota(jnp.int32, (hv, hv), 0) // GLA_DV
    hcol = lax.broadcasted_iota(jnp.int32, (hv, hv), 1) // GLA_DV
    head_ones = jnp.where(hrow == hcol, 1.0, 0.0).astype(BF16)
    ft = min(TOKEN_TILE, seq)

    def finish(i, carry):
        rows = pl.ds(pl.multiple_of(i * ft, ft), ft)
        o = acc_ref[rows, :]
        sq = o * o
        hi = sq.astype(BF16)
        lo = (sq - hi.astype(F32)).astype(BF16)
        ms = (_dot(hi, head_ones) + _dot(lo, head_ones)) * (1.0 / GLA_DV)
        on = o * lax.rsqrt(ms + EPS) * ghead_ref[...]
        gg = gg_ref[rows, :]
        yb = on * (gg * _sigmoid(gg))
        out_ref[rows, :] = _rms_norm(yb, gain_ref[...]).astype(BF16)
        return carry

    lax.fori_loop(0, seq // ft, finish, 0)


def _gla(q, k, v, laf, lab, gg, ghead, gain, bsz, seq):
    hk, hv = GLA_HEADS * GLA_DK, GLA_HEADS * GLA_DV
    seq_spec = lambda n: pl.BlockSpec((seq, n), lambda b: (b, 0))
    return pl.pallas_call(
        functools.partial(_gla_kernel, seq=seq), grid=(bsz,),
        in_specs=[seq_spec(hk), seq_spec(hk), seq_spec(hv), seq_spec(hk), seq_spec(hk), seq_spec(hv),
                  _const_spec((1, hv)), _const_spec((1, hv))],
        out_specs=seq_spec(hv), out_shape=jax.ShapeDtypeStruct((bsz * seq, hv), BF16),
        scratch_shapes=[pltpu.VMEM((seq, hv), F32), pltpu.VMEM((hk, hv), F32), pltpu.VMEM((hk, hv), F32)],
        compiler_params=_params("parallel"), name="gla",
    )(q, k, v, laf, lab, gg, ghead, gain)


def _attn_kernel(q_ref, k_ref, v_ref, gain_ref, out_ref, s0_ref, s1_ref, m_ref, acc_ref, *, seq, tk):
    n_pairs = seq // (2 * tk)
    heads = []
    for hh in range(MLA_HEADS):
        sl = slice(hh * HEAD_BLOCK, (hh + 1) * HEAD_BLOCK)

        def tile_rows(j):
            return pl.ds(pl.multiple_of(j * tk, tk), tk)

        def scores(dst_ref, j, sl=sl):
            dst_ref[...] = _dot_nt(q_ref[:, sl], k_ref[tile_rows(j), sl])

        def accumulate(src_ref, j, sl=sl):
            s = src_ref[...]
            m = m_ref[...]
            m_new = jnp.maximum(m, jnp.max(s, axis=1, keepdims=True))
            p = jnp.exp2(s - m_new)
            acc_ref[...] = acc_ref[...] * jnp.exp2(m - m_new) + _dot(p.astype(BF16), v_ref[tile_rows(j), sl])
            m_ref[...] = m_new

        def pair(jj, prefetch):
            scores(s1_ref, 2 * jj + 1)
            accumulate(s0_ref, 2 * jj)
            if prefetch:
                scores(s0_ref, 2 * jj + 2)
            accumulate(s1_ref, 2 * jj + 1)

        def body(jj, carry):
            pair(jj, True)
            return carry

        m_ref[...] = jnp.full_like(m_ref, -jnp.inf)
        acc_ref[...] = jnp.zeros_like(acc_ref)
        scores(s0_ref, 0)
        lax.fori_loop(0, n_pairs - 1, body, 0)
        pair(n_pairs - 1, False)
        acc = acc_ref[...]
        heads.append(acc[:, :MLA_V_DIM] / acc[:, MLA_V_DIM:MLA_V_DIM + 1])
    y = jnp.concatenate(heads, axis=1)
    out_ref[...] = _rms_norm(y, gain_ref[...]).astype(BF16)


def _attention(qc, kc, vc, gain, bsz, seq):
    tq, tk = min(ATTN_TQ, seq), min(ATTN_TK, seq)
    nq = seq // tq
    hb4 = MLA_HEADS * HEAD_BLOCK
    kv_spec = pl.BlockSpec((seq, hb4), lambda b, i: (b, 0))
    return pl.pallas_call(
        functools.partial(_attn_kernel, seq=seq, tk=tk), grid=(bsz, nq),
        in_specs=[pl.BlockSpec((tq, hb4), lambda b, i: (b * nq + i, 0)), kv_spec, kv_spec,
                  _const_spec((1, W_GRP))],
        out_specs=pl.BlockSpec((tq, W_GRP), lambda b, i: (b * nq + i, 0)),
        out_shape=jax.ShapeDtypeStruct((bsz * seq, W_GRP), BF16),
        scratch_shapes=[pltpu.VMEM((tq, tk), F32), pltpu.VMEM((tq, tk), F32),
                        pltpu.VMEM((tq, 1), F32), pltpu.VMEM((tq, HEAD_BLOCK), F32)],
        compiler_params=_params("parallel", "parallel"), name="mla_attention",
    )(qc, kc, vc, gain)


def _outproj_kernel(ab_ref, ap_ref, app_ref, apn_ref, du_ref, dup_ref, dun_ref, yb_ref, yc_ref, h_ref,
                    wsc_ref, wcfm_ref, gcfm_ref, bcfm_ref, gain_ref, wout_ref, g1_ref, b1_ref,
                    out_ref, pext_ref, uext_ref, *, tiles_per_seq):
    tm = ab_ref.shape[0]
    i = pl.program_id(0) % tiles_per_seq
    not_first = (i != 0).astype(F32)
    not_last = (i != tiles_per_seq - 1).astype(F32)
    pext_ref[0:HALO, :] = app_ref[...] * not_first
    pext_ref[HALO:HALO + tm, :] = ap_ref[...]
    pext_ref[HALO + tm:, :] = apn_ref[...] * not_last
    uext_ref[0:HALO, :] = dup_ref[...] * not_first
    uext_ref[HALO:HALO + tm, :] = du_ref[...]
    uext_ref[HALO + tm:, :] = dun_ref[...] * not_last

    conv = jnp.zeros((tm, W_GRP), F32)
    for j in range(SC_WIDTH):
        conv = conv + pext_ref[pl.ds(HALO - SC_WIDTH // 2 + j, tm), :] * wsc_ref[j:j + 1, :]
    ya = _rms_norm(ab_ref[...] * conv, gain_ref[:, 0:W_GRP])
    u = jnp.zeros((tm, W_GRP), F32)
    for j in range(CFM_WIDTH):
        u = u + uext_ref[pl.ds(HALO - CFM_WIDTH // 2 + j, tm), :] * wcfm_ref[j:j + 1, :]
    un = _layer_norm(u, gcfm_ref[...], bcfm_ref[...])
    yd = _rms_norm(un * _sigmoid(un), gain_ref[:, 3 * W_GRP:4 * W_GRP])

    mix = (_dot(ya.astype(BF16), wout_ref[0:W_GRP, :])
           + _dot(yb_ref[...], wout_ref[W_GRP:2 * W_GRP, :])
           + _dot(yc_ref[...], wout_ref[2 * W_GRP:3 * W_GRP, :])
           + _dot(yd.astype(BF16), wout_ref[3 * W_GRP:4 * W_GRP, :]))
    out_ref[...] = _layer_norm(DN_ALPHA * h_ref[...] + mix, g1_ref[...], b1_ref[...])


def _halo_specs(tm, cols, halo, total_rows):
    per = tm // halo
    last = total_rows // halo - 1
    prev = pl.BlockSpec((halo, cols), lambda i: (jnp.maximum(i * per - 1, 0), 0))
    nxt = pl.BlockSpec((halo, cols), lambda i: (jnp.minimum((i + 1) * per, last), 0))
    return prev, nxt


def _outproj(ab, ap, du, yb, yc, h, lw, seq, tm):
    t = h.shape[0]
    prev, nxt = _halo_specs(tm, W_GRP, HALO, t)
    g = _row_spec(tm, W_GRP)
    in_specs = [g, g, prev, nxt, g, prev, nxt, g, g, _row_spec(tm, D_MODEL),
                _const_spec((SC_WIDTH, W_GRP)), _const_spec((CFM_WIDTH, W_GRP)),
                _const_spec((1, W_GRP)), _const_spec((1, W_GRP)), _const_spec((1, 4 * W_GRP)),
                _const_spec((4 * W_GRP, D_MODEL)), _const_spec((1, D_MODEL)), _const_spec((1, D_MODEL))]
    return pl.pallas_call(
        functools.partial(_outproj_kernel, tiles_per_seq=seq // tm), grid=(t // tm,),
        in_specs=in_specs, out_specs=_row_spec(tm, D_MODEL),
        out_shape=jax.ShapeDtypeStruct((t, D_MODEL), F32),
        scratch_shapes=[pltpu.VMEM((tm + 2 * HALO, W_GRP), F32), pltpu.VMEM((tm + 2 * HALO, W_GRP), F32)],
        compiler_params=_params("parallel"), name="outproj",
    )(ab, ap, ap, ap, du, du, du, yb, yc, h, lw["w_sc_conv"], lw["w_cfm_dw"], lw["g_cfm_ln"],
      lw["b_cfm_ln"], lw["g_branch"], lw["w_out"], lw["ln1_g"], lw["ln1_b"])


def _ffn_kernel(h_ref, hp_ref, hn_ref, wup_ref, wdw_ref, wdown_ref, g2_ref, b2_ref, out_ref,
                hx_ref, acc_ref, ug0_ref, uv0_ref, ug1_ref, uv1_ref, *, tiles_per_seq, n_chunks):
    tm = h_ref.shape[0]
    i = pl.program_id(0) % tiles_per_seq
    not_first = (i != 0).astype(F32)
    not_last = (i != tiles_per_seq - 1).astype(F32)
    ext = tm + 2 * FFN_HALO
    hx_ref[0:FFN_HALO, :] = (hp_ref[...] * not_first).astype(BF16)
    hx_ref[FFN_HALO:FFN_HALO + tm, :] = h_ref[...].astype(BF16)
    hx_ref[FFN_HALO + tm:, :] = (hn_ref[...] * not_last).astype(BF16)
    acc_ref[...] = jnp.zeros_like(acc_ref)

    def conv3(u, w):
        prev = pltpu.roll(u, 1, 0)
        nxt = pltpu.roll(u, ext - 1, 0)
        full = prev * w[0:1, :] + u * w[1:2, :] + nxt * w[2:3, :]
        return full[FFN_HALO:FFN_HALO + tm, :]

    bufs = ((ug0_ref, uv0_ref), (ug1_ref, uv1_ref))

    def up(slot, c):
        hx = hx_ref[...]
        bufs[slot][0][...] = _dot(hx, wup_ref[0, c])
        bufs[slot][1][...] = _dot(hx, wup_ref[1, c])

    def activation(slot, c):
        gate = conv3(bufs[slot][0][...], wdw_ref[0, c])
        val = conv3(bufs[slot][1][...], wdw_ref[1, c])
        return (gate * _sigmoid(gate) * val).astype(BF16)

    def down_rows(c, n):
        return wdown_ref[pl.ds(pl.multiple_of(c * FF_CHUNK, FF_CHUNK), n * FF_CHUNK), :]

    def body(jj, carry):
        c = 2 * jj
        up(1, c + 1)
        a0 = activation(0, c)
        up(0, c + 2)
        a1 = activation(1, c + 1)
        acc_ref[...] += _dot(jnp.concatenate([a0, a1], axis=1), down_rows(c, 2))
        return carry

    assert n_chunks % 2 == 1
    up(0, 0)
    lax.fori_loop(0, n_chunks // 2, body, 0)
    last = n_chunks - 1
    mix = acc_ref[...] + _dot(activation(0, last), down_rows(last, 1))
    out_ref[...] = _layer_norm(DN_ALPHA * h_ref[...] + mix, g2_ref[...], b2_ref[...])


def _ffn(h, lw, seq, tm):
    t = h.shape[0]
    n_chunks = D_FF // FF_CHUNK
    prev, nxt = _halo_specs(tm, D_MODEL, FFN_HALO, t)
    in_specs = [_row_spec(tm, D_MODEL), prev, nxt,
                _const_spec((2, n_chunks, D_MODEL, FF_CHUNK)), _const_spec((2, n_chunks, FFN_CONV_WIDTH, FF_CHUNK)),
                _const_spec((D_FF, D_MODEL)), _const_spec((1, D_MODEL)), _const_spec((1, D_MODEL))]
    ext = tm + 2 * FFN_HALO
    return pl.pallas_call(
        functools.partial(_ffn_kernel, tiles_per_seq=seq // tm, n_chunks=n_chunks), grid=(t // tm,),
        in_specs=in_specs, out_specs=_row_spec(tm, D_MODEL),
        out_shape=jax.ShapeDtypeStruct((t, D_MODEL), F32),
        scratch_shapes=[pltpu.VMEM((ext, D_MODEL), BF16), pltpu.VMEM((tm, D_MODEL), F32)]
        + [pltpu.VMEM((ext, FF_CHUNK), F32)] * 4,
        compiler_params=_params("parallel"), name="conv_ffn",
    )(h, h, h, lw["w_ffn_up"], lw["w_ffn_dw"], lw["w_ffn_down"], lw["ln2_g"], lw["ln2_b"])


def _head_blocks(w, per_head, lo, n):
    rows = w.shape[0]
    wh = w.reshape(rows, -1, per_head)[:, :, lo:lo + n]
    return jnp.pad(wh, ((0, 0), (0, 0), (0, HEAD_BLOCK - n))).reshape(rows, -1)


def _layer_weights(p, l):
    w_in = p["w_in"][l]
    sizes = (256, 256, 256, 128, 128, 256, 256, 16, 16, 256, 128, 32, 512)
    offs = [0]
    for s in sizes:
        offs.append(offs[-1] + s)
    seg = lambda idx: w_in[:, offs[idx]:offs[idx + 1]]
    zeros = lambda n: jnp.zeros((D_MODEL, n), F32)
    misc = jnp.concatenate([seg(7), seg(8), zeros(ROPE_LO - 2 * GLA_LOWRANK), seg(11),
                            zeros(LANES - ROPE_LO - MLA_ROPE)], axis=1)
    w_in_p = jnp.concatenate([seg(0), seg(1), seg(2), seg(3), seg(4), seg(5), seg(6), misc, seg(9), seg(10),
                              seg(12)], axis=1).astype(BF16)
    hk = GLA_HEADS * GLA_DK
    w_gate = jnp.zeros((LANES, 2 * hk), F32)
    w_gate = w_gate.at[0:GLA_LOWRANK, 0:hk].set(p["w_gla_a_up"][l, 0])
    w_gate = w_gate.at[GLA_LOWRANK:2 * GLA_LOWRANK, hk:].set(p["w_gla_a_up"][l, 1])
    w_uq = p["w_mla_uq"][l]
    w_ukv = p["w_mla_ukv"][l]
    n_ff = D_FF // FF_CHUNK
    w_up = p["w_ffn_up"][l].reshape(D_MODEL, 2, n_ff, FF_CHUNK).transpose(1, 2, 0, 3)
    w_dw = p["w_ffn_dw"][l].reshape(FFN_CONV_WIDTH, 2, n_ff, FF_CHUNK).transpose(1, 2, 0, 3)
    row = lambda a: a.reshape(1, -1)
    return {
        "w_in": w_in_p,
        "w_gate": w_gate.astype(BF16),
        "b_gate": row(p["b_gla_a"][l]),
        "g_mla_q": row(p["g_mla_q"][l]),
        "w_uq": _head_blocks(w_uq, MLA_NOPE + MLA_ROPE, 0, MLA_NOPE + MLA_ROPE).astype(BF16),
        "g_mla_kv": row(p["g_mla_kv"][l]),
        "w_uk": _head_blocks(w_ukv, MLA_NOPE + MLA_V_DIM, 0, MLA_NOPE).astype(BF16),
        "w_uv": _head_blocks(w_ukv, MLA_NOPE + MLA_V_DIM, MLA_NOPE, MLA_V_DIM).astype(BF16),
        "g_gla_head": row(p["g_gla_head"][l]),
        "w_sc_conv": p["w_sc_conv"][l],
        "w_cfm_dw": p["w_cfm_dw"][l],
        "g_cfm_ln": row(p["g_cfm_ln"][l]),
        "b_cfm_ln": row(p["b_cfm_ln"][l]),
        "g_branch": row(p["g_branch"][l]),
        "w_out": p["w_out"][l].astype(BF16),
        "ln1_g": row(p["ln1_g"][l]),
        "ln1_b": row(p["ln1_b"][l]),
        "w_ffn_up": w_up.astype(BF16),
        "w_ffn_dw": w_dw,
        "w_ffn_down": p["w_ffn_down"][l].astype(BF16),
        "ln2_g": row(p["ln2_g"][l]),
        "ln2_b": row(p["ln2_b"][l]),
    }


def kernel(x, positions, ln_in_g, ln_in_b, w_in, w_sc_conv, w_gla_a_up, b_gla_a, g_gla_head, g_mla_q, w_mla_uq,
           g_mla_kv, w_mla_ukv, w_cfm_dw, g_cfm_ln, b_cfm_ln, g_branch, w_out, ln1_g, ln1_b, w_ffn_up, w_ffn_dw,
           w_ffn_down, ln2_g, ln2_b):
    p = dict(w_in=w_in, w_sc_conv=w_sc_conv, w_gla_a_up=w_gla_a_up, b_gla_a=b_gla_a, g_gla_head=g_gla_head,
             g_mla_q=g_mla_q, w_mla_uq=w_mla_uq, g_mla_kv=g_mla_kv, w_mla_ukv=w_mla_ukv, w_cfm_dw=w_cfm_dw,
             g_cfm_ln=g_cfm_ln, b_cfm_ln=b_cfm_ln, g_branch=g_branch, w_out=w_out, ln1_g=ln1_g, ln1_b=ln1_b,
             w_ffn_up=w_ffn_up, w_ffn_dw=w_ffn_dw, w_ffn_down=w_ffn_down, ln2_g=ln2_g, ln2_b=ln2_b)
    bsz, seq, d = x.shape
    assert d == D_MODEL and seq % GLA_CHUNK == 0
    tm = min(TOKEN_TILE, seq)
    assert seq % tm == 0 and tm % HALO == 0
    t = bsz * seq
    rope = _rope_tables(positions, tm)
    h = _input_norm(x.reshape(t, d), ln_in_g, ln_in_b, tm)
    for l in range(DEPTH):
        lw = _layer_weights(p, l)
        ab, ap, du, q, k, v, gg, laf, lab, qc, kc, vc = _inproj(h, lw, rope, tm)
        gain = lw["g_branch"]
        yb = _gla(q, k, v, laf, lab, gg, lw["g_gla_head"], gain[:, W_GRP:2 * W_GRP], bsz, seq)
        yc = _attention(qc, kc, vc, gain[:, 2 * W_GRP:3 * W_GRP], bsz, seq)
        h = _outproj(ab, ap, du, yb, yc, h, lw, seq, tm)
        h = _ffn(h, lw, seq, tm)
    return h.reshape(bsz, seq, d)
```

```python
import functools
import math

import jax
import jax.numpy as jnp
from jax import lax
from jax.experimental import pallas as pl
from jax.experimental.pallas import tpu as pltpu

F32 = jnp.float32
BF16 = jnp.bfloat16

D_MODEL = 1024
DEPTH = 2
W_GRP = 256
SC_WIDTH = 3
GLA_HEADS = 4
GLA_DK = 32
GLA_DV = 64
GLA_LOWRANK = 16
GLA_TAU = 16.0
MLA_HEADS = 4
MLA_V_DIM = 64
MLA_NOPE = 64
MLA_ROPE = 32
MLA_Q_RANK = 256
MLA_KV_RANK = 128
ROPE_BASE = 10000.0
CFM_WIDTH = 31
D_FF = 2816
FFN_CONV_WIDTH = 3
DN_ALPHA = (2.0 * DEPTH) ** 0.25
EPS = 1e-5

LANES = 128
SUBLANES = 8
VMEM_LIMIT_BYTES = 56 * 1024 * 1024

TOKEN_TILE = 512
FFN_TILE = 1024
HALO = 16
FFN_HALO = SUBLANES
FF_CHUNK = 256
GLA_CHUNK = 64
GLA_REF_ROW = GLA_CHUNK // 2
GLA_GROUP = 4
ATTN_TQ = 512
ATTN_TK = 512
HEAD_BLOCK = LANES
ROPE_LO = MLA_NOPE
ROPE_HALF = MLA_ROPE // 2
LOG2E = math.log2(math.e)

COL_SC_B, COL_SC_C, COL_SC_H = 0, 256, 512
COL_GQK, COL_GV, COL_GG = 768, 1024, 1280
COL_MISC_CKV = 1536
COL_CQ = 1792
COL_CFM_A, COL_CFM_G = 2048, 2304
D_IN_PAD = 2560


def _params(*sem):
    return pltpu.CompilerParams(dimension_semantics=sem, vmem_limit_bytes=VMEM_LIMIT_BYTES)


def _dot(a, b):
    return jnp.dot(a, b, preferred_element_type=F32)


def _dot_nt(a, b):
    return lax.dot_general(a, b, (((1,), (1,)), ((), ())), preferred_element_type=F32)


def _dot_tn(a, b):
    return lax.dot_general(a, b, (((0,), (0,)), ((), ())), preferred_element_type=F32)


def _layer_norm(x, g, b):
    mu = jnp.mean(x, axis=-1, keepdims=True)
    xc = x - mu
    var = jnp.mean(xc * xc, axis=-1, keepdims=True)
    return xc * lax.rsqrt(var + EPS) * g + b


def _rms_norm(x, g):
    return x * lax.rsqrt(jnp.mean(x * x, axis=-1, keepdims=True) + EPS) * g


def _sigmoid(x):
    return 1.0 / (1.0 + jnp.exp(-x))


def _log_sigmoid(x):
    return jnp.minimum(x, 0.0) - jnp.log1p(jnp.exp(-jnp.abs(x)))


def _const_spec(shape):
    return pl.BlockSpec(shape, lambda *_: (0,) * len(shape), pipeline_mode=pl.Buffered(1))


def _row_spec(rows, cols):
    return pl.BlockSpec((rows, cols), lambda i: (i, 0))


def _rope_kernel(pos_ref, inv_ref, c_ref, s1_ref, s2_ref):
    ang = pos_ref[...].astype(F32) * inv_ref[...]
    lane = lax.broadcasted_iota(jnp.int32, ang.shape, 1)
    first = (lane >= ROPE_LO) & (lane < ROPE_LO + ROPE_HALF)
    second = (lane >= ROPE_LO + ROPE_HALF) & (lane < ROPE_LO + MLA_ROPE)
    cos, sin = jnp.cos(ang), jnp.sin(ang)
    c_ref[...] = jnp.where(lane < ROPE_LO, 1.0, jnp.where(first | second, cos, 0.0))
    s1_ref[...] = jnp.where(second, sin, 0.0)
    s2_ref[...] = jnp.where(first, -sin, 0.0)


def _rope_tables(positions, tm):
    t = positions.size
    inv = ROPE_BASE ** (-jnp.arange(0, MLA_ROPE, 2, dtype=F32) / MLA_ROPE)
    inv_lane = jnp.zeros((1, LANES), F32)
    inv_lane = inv_lane.at[0, ROPE_LO:ROPE_LO + ROPE_HALF].set(inv)
    inv_lane = inv_lane.at[0, ROPE_LO + ROPE_HALF:ROPE_LO + MLA_ROPE].set(inv)
    out = jax.ShapeDtypeStruct((t, LANES), F32)
    return pl.pallas_call(
        _rope_kernel, grid=(t // tm,),
        in_specs=[_row_spec(tm, 1), _const_spec((1, LANES))],
        out_specs=[_row_spec(tm, LANES)] * 3, out_shape=[out] * 3,
        compiler_params=_params("parallel"), name="rope_tables",
    )(positions.reshape(t, 1), inv_lane)


def _rope(x, c, s1, s2):
    return (x * c + pltpu.roll(x, ROPE_HALF, 1) * s1
            + pltpu.roll(x, LANES - ROPE_HALF, 1) * s2)


def _ln_kernel(x_ref, g_ref, b_ref, o_ref):
    o_ref[...] = _layer_norm(x_ref[...], g_ref[...], b_ref[...])


def _input_norm(x, g, b, tm):
    t, d = x.shape
    return pl.pallas_call(
        _ln_kernel, grid=(t // tm,),
        in_specs=[_row_spec(tm, d), _const_spec((1, d)), _const_spec((1, d))],
        out_specs=_row_spec(tm, d), out_shape=jax.ShapeDtypeStruct((t, d), F32),
        compiler_params=_params("parallel"), name="input_norm",
    )(x, g.reshape(1, d), b.reshape(1, d))


def _inproj_kernel(h_ref, w_ref, wg_ref, bg_ref, gq_ref, wuq_ref, gkv_ref, wuk_ref, wuv_ref,
                   c_ref, s1_ref, s2_ref,
                   ab_ref, ap_ref, du_ref, q_ref, k_ref, v_ref, gg_ref, laf_ref, lab_ref,
                   qc_ref, kc_ref, vc_ref):
    hb = h_ref[...].astype(BF16)

    def proj(lo, n):
        return _dot(hb, w_ref[:, lo:lo + n])

    ab_ref[...] = proj(COL_SC_B, W_GRP)
    ap_ref[...] = proj(COL_SC_C, W_GRP) * proj(COL_SC_H, W_GRP)
    du_ref[...] = proj(COL_CFM_A, W_GRP) * _sigmoid(proj(COL_CFM_G, W_GRP))
    qk = proj(COL_GQK, 2 * LANES)
    q_ref[...] = qk[:, :LANES] * (GLA_DK ** -0.5)
    k_ref[...] = qk[:, LANES:]
    v_ref[...] = proj(COL_GV, W_GRP)
    gg_ref[...] = proj(COL_GG, W_GRP)
    misc_ckv = proj(COL_MISC_CKV, 2 * LANES)
    misc = misc_ckv[:, :LANES]
    pre = _dot(misc.astype(BF16), wg_ref[...]) + bg_ref[...]
    la = _log_sigmoid(pre) * (1.0 / GLA_TAU)
    laf_ref[...] = la[:, :LANES]
    lab_ref[...] = la[:, LANES:]
    c, s1, s2 = c_ref[...], s1_ref[...], s2_ref[...]
    cqn = _rms_norm(proj(COL_CQ, MLA_Q_RANK), gq_ref[...]).astype(BF16)
    qm = _dot(cqn, wuq_ref[...])
    qscale = (MLA_NOPE + MLA_ROPE) ** -0.5 * LOG2E
    ckvn = _rms_norm(misc_ckv[:, LANES:], gkv_ref[...]).astype(BF16)
    kn = _dot(ckvn, wuk_ref[...])
    vn = _dot(ckvn, wuv_ref[...])
    lane = lax.broadcasted_iota(jnp.int32, misc.shape, 1)
    kr = _rope(jnp.where(lane >= ROPE_LO, misc, 0.0), c, s1, s2)
    ones_col = jnp.where(lane == MLA_V_DIM, 1.0, 0.0)
    for hh in range(MLA_HEADS):
        sl = slice(hh * HEAD_BLOCK, (hh + 1) * HEAD_BLOCK)
        qc_ref[:, sl] = (_rope(qm[:, sl], c, s1, s2) * qscale).astype(BF16)
        kc_ref[:, sl] = (kn[:, sl] + kr).astype(BF16)
        vc_ref[:, sl] = (vn[:, sl] + ones_col).astype(BF16)


def _inproj(h, lw, rope, tm):
    t = h.shape[0]
    f = lambda n, dt=F32: jax.ShapeDtypeStruct((t, n), dt)
    hb4 = MLA_HEADS * HEAD_BLOCK
    out_shape = [f(W_GRP), f(W_GRP), f(W_GRP), f(LANES), f(LANES), f(W_GRP), f(W_GRP), f(LANES), f(LANES),
                 f(hb4, BF16), f(hb4, BF16), f(hb4, BF16)]
    out_specs = [_row_spec(tm, s.shape[1]) for s in out_shape]
    in_specs = [_row_spec(tm, D_MODEL), _const_spec((D_MODEL, D_IN_PAD)),
                _const_spec((LANES, 2 * LANES)), _const_spec((1, 2 * LANES)),
                _const_spec((1, MLA_Q_RANK)), _const_spec((MLA_Q_RANK, hb4)),
                _const_spec((1, MLA_KV_RANK)), _const_spec((MLA_KV_RANK, hb4)), _const_spec((MLA_KV_RANK, hb4)),
                _row_spec(tm, LANES), _row_spec(tm, LANES), _row_spec(tm, LANES)]
    return pl.pallas_call(
        _inproj_kernel, grid=(t // tm,), in_specs=in_specs, out_specs=out_specs, out_shape=out_shape,
        compiler_params=_params("parallel"), name="inproj",
    )(h, lw["w_in"], lw["w_gate"], lw["b_gate"], lw["g_mla_q"], lw["w_uq"], lw["g_mla_kv"], lw["w_uk"],
      lw["w_uv"], *rope)


def _gla_kernel(q_ref, k_ref, v_ref, laf_ref, lab_ref, gg_ref, ghead_ref, gain_ref, out_ref,
                acc_ref, sf_ref, sb_ref, *, seq):
    cs = GLA_CHUNK
    grp = min(GLA_GROUP, seq // cs)
    gr = grp * cs
    n_groups = seq // gr
    hk = GLA_HEADS * GLA_DK
    hv = GLA_HEADS * GLA_DV
    row = lax.broadcasted_iota(jnp.int32, (gr, gr), 0)
    col = lax.broadcasted_iota(jnp.int32, (gr, gr), 1)
    tri = jnp.where((row >= col) & (row // cs == col // cs), 1.0, 0.0).astype(BF16)
    srow = lax.broadcasted_iota(jnp.int32, (GLA_HEADS * cs, cs), 0) % cs
    scol = lax.broadcasted_iota(jnp.int32, (GLA_HEADS * cs, cs), 1)
    keep = (srow >= scol, srow < scol)
    qlane_head = lax.broadcasted_iota(jnp.int32, (cs, hk), 1) // GLA_DK
    olane_head = lax.broadcasted_iota(jnp.int32, (cs, hv), 1) // GLA_DV
    srow_head = lax.broadcasted_iota(jnp.int32, (hk, hv), 0) // GLA_DK
    scol_head = lax.broadcasted_iota(jnp.int32, (hk, hv), 1) // GLA_DV
    eye = (lax.broadcasted_iota(jnp.int32, (hk, hk), 0) == lax.broadcasted_iota(jnp.int32, (hk, hk), 1))

    acc_ref[...] = jnp.zeros_like(acc_ref)
    sf_ref[...] = jnp.zeros_like(sf_ref)
    sb_ref[...] = jnp.zeros_like(sb_ref)

    def group_step(g, direction):
        la_ref, s_ref = ((laf_ref, sf_ref), (lab_ref, sb_ref))[direction]
        base = pl.multiple_of(g * gr, gr)
        rows = pl.ds(base, gr)
        la = la_ref[rows, :]
        hi = la.astype(BF16)
        rem = la - hi.astype(F32)
        mid = rem.astype(BF16)
        lo = (rem - mid.astype(F32)).astype(BF16)
        cum = _dot(tri, hi) + _dot(tri, mid) + _dot(tri, lo)
        q, k = q_ref[rows, :], k_ref[rows, :]
        vb = v_ref[rows, :].astype(BF16)
        p, qin, vbs, kv, decay = [], [], [], [], []
        for i in range(grp):
            sl = slice(i * cs, (i + 1) * cs)
            total = cum[(i + 1) * cs - 1:(i + 1) * cs, :]
            b = cum[sl] if direction == 0 else total - cum[sl] + la[sl]
            ref = b[GLA_REF_ROW:GLA_REF_ROW + 1, :]
            qt = q[sl] * jnp.exp(b - ref)
            kt = k[sl] * jnp.exp(ref - b)
            qbd = jnp.concatenate([jnp.where(qlane_head == hh, qt, 0.0) for hh in range(GLA_HEADS)], axis=0)
            scores = _dot_nt(qbd.astype(BF16), kt.astype(BF16))
            p.append(jnp.where(keep[direction], scores, 0.0).astype(BF16))
            qin.append((qbd * jnp.exp(ref)).astype(BF16))
            khat = (kt * jnp.exp(total - ref)).astype(BF16)
            vbs.append(vb[sl])
            kv.append(jnp.where(srow_head == scol_head, _dot_tn(khat, vb[sl]), 0.0))
            decay.append(jnp.sum(jnp.where(eye, jnp.broadcast_to(jnp.exp(total), (hk, hk)), 0.0),
                                 axis=1, keepdims=True))
        state = s_ref[...]
        for i in (range(grp) if direction == 0 else reversed(range(grp))):
            o_all = _dot(p[i], vbs[i]) + _dot(qin[i], state.astype(BF16))
            o = jnp.zeros((cs, hv), F32)
            for hh in range(GLA_HEADS):
                o = o + jnp.where(olane_head == hh, o_all[hh * cs:(hh + 1) * cs, :], 0.0)
            acc_ref[pl.ds(base + i * cs, cs), :] += o
            state = state * decay[i] + kv[i]
        s_ref[...] = state

    def body(n, carry):
        group_step(n, 0)
        group_step(n_groups - 1 - n, 1)
        return carry

    lax.fori_loop(0, n_groups, body, 0)

    hrow = lax.broadcasted_iota(jnp.int32, (hv, hv), 0) // GLA_DV
    hcol = lax.broadcasted_iota(jnp.int32, (hv, hv), 1) // GLA_DV
    head_ones = jnp.where(hrow == hcol, 1.0, 0.0).astype(BF16)
    ft = min(TOKEN_TILE, seq)

    def finish(i, carry):
        rows = pl.ds(pl.multiple_of(i * ft, ft), ft)
        o = acc_ref[rows, :]
        sq = o * o
        hi = sq.astype(BF16)
        lo = (sq - hi.astype(F32)).astype(BF16)
        ms = (_dot(hi, head_ones) + _dot(lo, head_ones)) * (1.0 / GLA_DV)
        on = o * lax.rsqrt(ms + EPS) * ghead_ref[...]
        gg = gg_ref[rows, :]
        yb = on * (gg * _sigmoid(gg))
        out_ref[rows, :] = _rms_norm(yb, gain_ref[...]).astype(BF16)
        return carry

    lax.fori_loop(0, seq // ft, finish, 0)


def _gla(q, k, v, laf, lab, gg, ghead, gain, bsz, seq):
    hk, hv = GLA_HEADS * GLA_DK, GLA_HEADS * GLA_DV
    seq_spec = lambda n: pl.BlockSpec((seq, n), lambda b: (b, 0))
    return pl.pallas_call(
        functools.partial(_gla_kernel, seq=seq), grid=(bsz,),
        in_specs=[seq_spec(hk), seq_spec(hk), seq_spec(hv), seq_spec(hk), seq_spec(hk), seq_spec(hv),
                  _const_spec((1, hv)), _const_spec((1, hv))],
        out_specs=seq_spec(hv), out_shape=jax.ShapeDtypeStruct((bsz * seq, hv), BF16),
        scratch_shapes=[pltpu.VMEM((seq, hv), F32), pltpu.VMEM((hk, hv), F32), pltpu.VMEM((hk, hv), F32)],
        compiler_params=_params("parallel"), name="gla",
    )(q, k, v, laf, lab, gg, ghead, gain)


def _attn_kernel(q_ref, k_ref, v_ref, gain_ref, out_ref, s0_ref, s1_ref, m_ref, acc_ref, *, seq, tk):
    n_pairs = seq // (2 * tk)
    heads = []
    for hh in range(MLA_HEADS):
        sl = slice(hh * HEAD_BLOCK, (hh + 1) * HEAD_BLOCK)

        def tile_rows(j):
            return pl.ds(pl.multiple_of(j * tk, tk), tk)

        def scores(dst_ref, j, sl=sl):
            dst_ref[...] = _dot_nt(q_ref[:, sl], k_ref[tile_rows(j), sl])

        def accumulate(src_ref, j, sl=sl):
            s = src_ref[...]
            m = m_ref[...]
            m_new = jnp.maximum(m, jnp.max(s, axis=1, keepdims=True))
            p = jnp.exp2(s - m_new)
            acc_ref[...] = acc_ref[...] * jnp.exp2(m - m_new) + _dot(p.astype(BF16), v_ref[tile_rows(j), sl])
            m_ref[...] = m_new

        def pair(jj, prefetch):
            scores(s1_ref, 2 * jj + 1)
            accumulate(s0_ref, 2 * jj)
            if prefetch:
                scores(s0_ref, 2 * jj + 2)
            accumulate(s1_ref, 2 * jj + 1)

        def body(jj, carry):
            pair(jj, True)
            return carry

        m_ref[...] = jnp.full_like(m_ref, -jnp.inf)
        acc_ref[...] = jnp.zeros_like(acc_ref)
        scores(s0_ref, 0)
        lax.fori_loop(0, n_pairs - 1, body, 0)
        pair(n_pairs - 1, False)
        acc = acc_ref[...]
        heads.append(acc[:, :MLA_V_DIM] / acc[:, MLA_V_DIM:MLA_V_DIM + 1])
    y = jnp.concatenate(heads, axis=1)
    out_ref[...] = _rms_norm(y, gain_ref[...]).astype(BF16)


def _attention(qc, kc, vc, gain, bsz, seq):
    tq, tk = min(ATTN_TQ, seq), min(ATTN_TK, seq)
    nq = seq // tq
    hb4 = MLA_HEADS * HEAD_BLOCK
    kv_spec = pl.BlockSpec((seq, hb4), lambda b, i: (b, 0))
    return pl.pallas_call(
        functools.partial(_attn_kernel, seq=seq, tk=tk), grid=(bsz, nq),
        in_specs=[pl.BlockSpec((tq, hb4), lambda b, i: (b * nq + i, 0)), kv_spec, kv_spec,
                  _const_spec((1, W_GRP))],
        out_specs=pl.BlockSpec((tq, W_GRP), lambda b, i: (b * nq + i, 0)),
        out_shape=jax.ShapeDtypeStruct((bsz * seq, W_GRP), BF16),
        scratch_shapes=[pltpu.VMEM((tq, tk), F32), pltpu.VMEM((tq, tk), F32),
                        pltpu.VMEM((tq, 1), F32), pltpu.VMEM((tq, HEAD_BLOCK), F32)],
        compiler_params=_params("parallel", "parallel"), name="mla_attention",
    )(qc, kc, vc, gain)


def _outproj_kernel(ab_ref, ap_ref, app_ref, apn_ref, du_ref, dup_ref, dun_ref, yb_ref, yc_ref, h_ref,
                    wsc_ref, wcfm_ref, gcfm_ref, bcfm_ref, gain_ref, wout_ref, g1_ref, b1_ref,
                    out_ref, pext_ref, uext_ref, shift_ref, *, tiles_per_seq):
    tm = ab_ref.shape[0]
    i = pl.program_id(0) % tiles_per_seq
    not_first = (i != 0).astype(F32)
    not_last = (i != tiles_per_seq - 1).astype(F32)
    pext_ref[0:HALO, :] = app_ref[...] * not_first
    pext_ref[HALO:HALO + tm, :] = ap_ref[...]
    pext_ref[HALO + tm:, :] = apn_ref[...] * not_last
    uext_ref[0:HALO, :] = dup_ref[...] * not_first
    uext_ref[HALO:HALO + tm, :] = du_ref[...]
    uext_ref[HALO + tm:, :] = dun_ref[...] * not_last

    conv = jnp.zeros((tm, W_GRP), F32)
    for j in range(SC_WIDTH):
        conv = conv + pext_ref[pl.ds(HALO - SC_WIDTH // 2 + j, tm), :] * wsc_ref[j:j + 1, :]
    ya = _rms_norm(ab_ref[...] * conv, gain_ref[:, 0:W_GRP])
    first = HALO - CFM_WIDTH // 2
    span = SUBLANES * ((first + CFM_WIDTH - 1) // SUBLANES)
    assert shift_ref.shape[0] == tm + span and span + SUBLANES <= 2 * HALO
    u = jnp.zeros((tm, W_GRP), F32)
    for r in range(SUBLANES):
        shift_ref[...] = uext_ref[pl.ds(r, tm + span), :]
        for off in range(0, span + 1, SUBLANES):
            j = off + r - first
            if 0 <= j < CFM_WIDTH:
                u = u + shift_ref[pl.ds(off, tm), :] * wcfm_ref[j:j + 1, :]
    un = _layer_norm(u, gcfm_ref[...], bcfm_ref[...])
    yd = _rms_norm(un * _sigmoid(un), gain_ref[:, 3 * W_GRP:4 * W_GRP])

    mix = (_dot(ya.astype(BF16), wout_ref[0:W_GRP, :])
           + _dot(yb_ref[...], wout_ref[W_GRP:2 * W_GRP, :])
           + _dot(yc_ref[...], wout_ref[2 * W_GRP:3 * W_GRP, :])
           + _dot(yd.astype(BF16), wout_ref[3 * W_GRP:4 * W_GRP, :]))
    out_ref[...] = _layer_norm(DN_ALPHA * h_ref[...] + mix, g1_ref[...], b1_ref[...])


def _halo_specs(tm, cols, halo, total_rows):
    per = tm // halo
    last = total_rows // halo - 1
    prev = pl.BlockSpec((halo, cols), lambda i: (jnp.maximum(i * per - 1, 0), 0))
    nxt = pl.BlockSpec((halo, cols), lambda i: (jnp.minimum((i + 1) * per, last), 0))
    return prev, nxt


def _outproj(ab, ap, du, yb, yc, h, lw, seq, tm):
    t = h.shape[0]
    prev, nxt = _halo_specs(tm, W_GRP, HALO, t)
    g = _row_spec(tm, W_GRP)
    in_specs = [g, g, prev, nxt, g, prev, nxt, g, g, _row_spec(tm, D_MODEL),
                _const_spec((SC_WIDTH, W_GRP)), _const_spec((CFM_WIDTH, W_GRP)),
                _const_spec((1, W_GRP)), _const_spec((1, W_GRP)), _const_spec((1, 4 * W_GRP)),
                _const_spec((4 * W_GRP, D_MODEL)), _const_spec((1, D_MODEL)), _const_spec((1, D_MODEL))]
    return pl.pallas_call(
        functools.partial(_outproj_kernel, tiles_per_seq=seq // tm), grid=(t // tm,),
        in_specs=in_specs, out_specs=_row_spec(tm, D_MODEL),
        out_shape=jax.ShapeDtypeStruct((t, D_MODEL), F32),
        scratch_shapes=[pltpu.VMEM((tm + 2 * HALO, W_GRP), F32), pltpu.VMEM((tm + 2 * HALO, W_GRP), F32),
                        pltpu.VMEM((tm + 2 * HALO - SUBLANES, W_GRP), F32)],
        compiler_params=_params("parallel"), name="outproj",
    )(ab, ap, ap, ap, du, du, du, yb, yc, h, lw["w_sc_conv"], lw["w_cfm_dw"], lw["g_cfm_ln"],
      lw["b_cfm_ln"], lw["g_branch"], lw["w_out"], lw["ln1_g"], lw["ln1_b"])


def _ffn_kernel(h_ref, hp_ref, hn_ref, wup_ref, wdw_ref, wdown_ref, g2_ref, b2_ref, out_ref,
                hx_ref, acc_ref, ug0_ref, uv0_ref, ug1_ref, uv1_ref, *, tiles_per_seq, n_chunks):
    tm = h_ref.shape[0]
    i = pl.program_id(0) % tiles_per_seq
    not_first = (i != 0).astype(F32)
    not_last = (i != tiles_per_seq - 1).astype(F32)
    ext = tm + 2 * FFN_HALO
    hx_ref[0:FFN_HALO, :] = (hp_ref[...] * not_first).astype(BF16)
    hx_ref[FFN_HALO:FFN_HALO + tm, :] = h_ref[...].astype(BF16)
    hx_ref[FFN_HALO + tm:, :] = (hn_ref[...] * not_last).astype(BF16)
    acc_ref[...] = jnp.zeros_like(acc_ref)

    def conv3(u, w):
        prev = pltpu.roll(u, 1, 0)
        nxt = pltpu.roll(u, ext - 1, 0)
        full = prev * w[0:1, :] + u * w[1:2, :] + nxt * w[2:3, :]
        return full[FFN_HALO:FFN_HALO + tm, :]

    bufs = ((ug0_ref, uv0_ref), (ug1_ref, uv1_ref))

    def up_cols(c, half):
        return wup_ref[:, pl.ds(pl.multiple_of(half * D_FF + c * FF_CHUNK, FF_CHUNK), FF_CHUNK)]

    def up(slot, c):
        hx = hx_ref[...]
        bufs[slot][0][...] = _dot(hx, up_cols(c, 0))
        bufs[slot][1][...] = _dot(hx, up_cols(c, 1))

    def activation(slot, c):
        gate = conv3(bufs[slot][0][...], wdw_ref[0, c])
        val = conv3(bufs[slot][1][...], wdw_ref[1, c])
        return (gate * _sigmoid(gate) * val).astype(BF16)

    def down_rows(c, n):
        return wdown_ref[pl.ds(pl.multiple_of(c * FF_CHUNK, FF_CHUNK), n * FF_CHUNK), :]

    def body(jj, carry):
        c = 2 * jj
        up(1, c + 1)
        a0 = activation(0, c)
        up(0, c + 2)
        a1 = activation(1, c + 1)
        acc_ref[...] += _dot(jnp.concatenate([a0, a1], axis=1), down_rows(c, 2))
        return carry

    assert n_chunks % 2 == 1
    up(0, 0)
    lax.fori_loop(0, n_chunks // 2, body, 0)
    last = n_chunks - 1
    mix = acc_ref[...] + _dot(activation(0, last), down_rows(last, 1))
    out_ref[...] = _layer_norm(DN_ALPHA * h_ref[...] + mix, g2_ref[...], b2_ref[...])


def _ffn(h, lw, seq, tm):
    t = h.shape[0]
    n_chunks = D_FF // FF_CHUNK
    prev, nxt = _halo_specs(tm, D_MODEL, FFN_HALO, t)
    in_specs = [_row_spec(tm, D_MODEL), prev, nxt,
                _const_spec((D_MODEL, 2 * D_FF)), _const_spec((2, n_chunks, FFN_CONV_WIDTH, FF_CHUNK)),
                _const_spec((D_FF, D_MODEL)), _const_spec((1, D_MODEL)), _const_spec((1, D_MODEL))]
    ext = tm + 2 * FFN_HALO
    return pl.pallas_call(
        functools.partial(_ffn_kernel, tiles_per_seq=seq // tm, n_chunks=n_chunks), grid=(t // tm,),
        in_specs=in_specs, out_specs=_row_spec(tm, D_MODEL),
        out_shape=jax.ShapeDtypeStruct((t, D_MODEL), F32),
        scratch_shapes=[pltpu.VMEM((ext, D_MODEL), BF16), pltpu.VMEM((tm, D_MODEL), F32)]
        + [pltpu.VMEM((ext, FF_CHUNK), F32)] * 4,
        compiler_params=_params("parallel"), name="conv_ffn",
    )(h, h, h, lw["w_ffn_up"], lw["w_ffn_dw"], lw["w_ffn_down"], lw["ln2_g"], lw["ln2_b"])


def _head_blocks(w, per_head, lo, n):
    rows = w.shape[0]
    wh = w.reshape(rows, -1, per_head)[:, :, lo:lo + n]
    return jnp.pad(wh, ((0, 0), (0, 0), (0, HEAD_BLOCK - n))).reshape(rows, -1)


def _layer_weights(p, l):
    w_in = p["w_in"][l]
    sizes = (256, 256, 256, 128, 128, 256, 256, 16, 16, 256, 128, 32, 512)
    offs = [0]
    for s in sizes:
        offs.append(offs[-1] + s)
    seg = lambda idx: w_in[:, offs[idx]:offs[idx + 1]]
    zeros = lambda n: jnp.zeros((D_MODEL, n), F32)
    misc = jnp.concatenate([seg(7), seg(8), zeros(ROPE_LO - 2 * GLA_LOWRANK), seg(11),
                            zeros(LANES - ROPE_LO - MLA_ROPE)], axis=1)
    w_in_p = jnp.concatenate([seg(0), seg(1), seg(2), seg(3), seg(4), seg(5), seg(6), misc, seg(10), seg(9),
                              seg(12)], axis=1).astype(BF16)
    hk = GLA_HEADS * GLA_DK
    w_gate = jnp.zeros((LANES, 2 * hk), F32)
    w_gate = w_gate.at[0:GLA_LOWRANK, 0:hk].set(p["w_gla_a_up"][l, 0])
    w_gate = w_gate.at[GLA_LOWRANK:2 * GLA_LOWRANK, hk:].set(p["w_gla_a_up"][l, 1])
    w_uq = p["w_mla_uq"][l]
    w_ukv = p["w_mla_ukv"][l]
    n_ff = D_FF // FF_CHUNK
    w_dw = p["w_ffn_dw"][l].reshape(FFN_CONV_WIDTH, 2, n_ff, FF_CHUNK).transpose(1, 2, 0, 3)
    row = lambda a: a.reshape(1, -1)
    return {
        "w_in": w_in_p,
        "w_gate": w_gate.astype(BF16),
        "b_gate": row(p["b_gla_a"][l]),
        "g_mla_q": row(p["g_mla_q"][l]),
        "w_uq": _head_blocks(w_uq, MLA_NOPE + MLA_ROPE, 0, MLA_NOPE + MLA_ROPE).astype(BF16),
        "g_mla_kv": row(p["g_mla_kv"][l]),
        "w_uk": _head_blocks(w_ukv, MLA_NOPE + MLA_V_DIM, 0, MLA_NOPE).astype(BF16),
        "w_uv": _head_blocks(w_ukv, MLA_NOPE + MLA_V_DIM, MLA_NOPE, MLA_V_DIM).astype(BF16),
        "g_gla_head": row(p["g_gla_head"][l]),
        "w_sc_conv": p["w_sc_conv"][l],
        "w_cfm_dw": p["w_cfm_dw"][l],
        "g_cfm_ln": row(p["g_cfm_ln"][l]),
        "b_cfm_ln": row(p["b_cfm_ln"][l]),
        "g_branch": row(p["g_branch"][l]),
        "w_out": p["w_out"][l].astype(BF16),
        "ln1_g": row(p["ln1_g"][l]),
        "ln1_b": row(p["ln1_b"][l]),
        "w_ffn_up": p["w_ffn_up"][l].astype(BF16),
        "w_ffn_dw": w_dw,
        "w_ffn_down": p["w_ffn_down"][l].astype(BF16),
        "ln2_g": row(p["ln2_g"][l]),
        "ln2_b": row(p["ln2_b"][l]),
    }


def kernel(x, positions, ln_in_g, ln_in_b, w_in, w_sc_conv, w_gla_a_up, b_gla_a, g_gla_head, g_mla_q, w_mla_uq,
           g_mla_kv, w_mla_ukv, w_cfm_dw, g_cfm_ln, b_cfm_ln, g_branch, w_out, ln1_g, ln1_b, w_ffn_up, w_ffn_dw,
           w_ffn_down, ln2_g, ln2_b):
    p = dict(w_in=w_in, w_sc_conv=w_sc_conv, w_gla_a_up=w_gla_a_up, b_gla_a=b_gla_a, g_gla_head=g_gla_head,
             g_mla_q=g_mla_q, w_mla_uq=w_mla_uq, g_mla_kv=g_mla_kv, w_mla_ukv=w_mla_ukv, w_cfm_dw=w_cfm_dw,
             g_cfm_ln=g_cfm_ln, b_cfm_ln=b_cfm_ln, g_branch=g_branch, w_out=w_out, ln1_g=ln1_g, ln1_b=ln1_b,
             w_ffn_up=w_ffn_up, w_ffn_dw=w_ffn_dw, w_ffn_down=w_ffn_down, ln2_g=ln2_g, ln2_b=ln2_b)
    bsz, seq, d = x.shape
    assert d == D_MODEL and seq % GLA_CHUNK == 0
    tm = min(TOKEN_TILE, seq)
    assert seq % tm == 0 and tm % HALO == 0
    t = bsz * seq
    rope = _rope_tables(positions, tm)
    h = _input_norm(x.reshape(t, d), ln_in_g, ln_in_b, tm)
    for l in range(DEPTH):
        lw = _layer_weights(p, l)
        ab, ap, du, q, k, v, gg, laf, lab, qc, kc, vc = _inproj(h, lw, rope, tm)
        gain = lw["g_branch"]
        yb = _gla(q, k, v, laf, lab, gg, lw["g_gla_head"], gain[:, W_GRP:2 * W_GRP], bsz, seq)
        yc = _attention(qc, kc, vc, gain[:, 2 * W_GRP:3 * W_GRP], bsz, seq)
        h = _outproj(ab, ap, du, yb, yc, h, lw, seq, tm)
        h = _ffn(h, lw, seq, min(FFN_TILE, seq))
    return h.reshape(bsz, seq, d)
```

```python
import functools
import math

import jax
import jax.numpy as jnp
from jax import lax
from jax.experimental import pallas as pl
from jax.experimental.pallas import tpu as pltpu

F32 = jnp.float32
BF16 = jnp.bfloat16

D_MODEL = 1024
DEPTH = 2
W_GRP = 256
SC_WIDTH = 3
GLA_HEADS = 4
GLA_DK = 32
GLA_DV = 64
GLA_LOWRANK = 16
GLA_TAU = 16.0
MLA_HEADS = 4
MLA_V_DIM = 64
MLA_NOPE = 64
MLA_ROPE = 32
MLA_Q_RANK = 256
MLA_KV_RANK = 128
ROPE_BASE = 10000.0
CFM_WIDTH = 31
D_FF = 2816
FFN_CONV_WIDTH = 3
DN_ALPHA = (2.0 * DEPTH) ** 0.25
EPS = 1e-5

LANES = 128
SUBLANES = 8
VMEM_LIMIT_BYTES = 56 * 1024 * 1024

TOKEN_TILE = 512
FFN_TILE = 1024
HALO = 16
FFN_HALO = SUBLANES
FF_CHUNK = 256
GLA_CHUNK = 64
GLA_REF_ROW = GLA_CHUNK // 2
GLA_GROUP = 4
ATTN_TQ = 512
ATTN_TK = 2048
HEAD_BLOCK = LANES
ROPE_LO = MLA_NOPE
ROPE_HALF = MLA_ROPE // 2
LOG2E = math.log2(math.e)

COL_SC_B, COL_SC_C, COL_SC_H = 0, 256, 512
COL_GQK, COL_GV, COL_GG = 768, 1024, 1280
COL_MISC_CKV = 1536
COL_CQ = 1792
COL_CFM_A, COL_CFM_G = 2048, 2304
D_IN_PAD = 2560


def _params(*sem):
    return pltpu.CompilerParams(dimension_semantics=sem, vmem_limit_bytes=VMEM_LIMIT_BYTES)


def _dot(a, b):
    return jnp.dot(a, b, preferred_element_type=F32)


def _dot_nt(a, b):
    return lax.dot_general(a, b, (((1,), (1,)), ((), ())), preferred_element_type=F32)


def _dot_tn(a, b):
    return lax.dot_general(a, b, (((0,), (0,)), ((), ())), preferred_element_type=F32)


def _layer_norm(x, g, b):
    mu = jnp.mean(x, axis=-1, keepdims=True)
    xc = x - mu
    var = jnp.mean(xc * xc, axis=-1, keepdims=True)
    return xc * lax.rsqrt(var + EPS) * g + b


def _rms_norm(x, g):
    return x * lax.rsqrt(jnp.mean(x * x, axis=-1, keepdims=True) + EPS) * g


def _sigmoid(x):
    return 1.0 / (1.0 + jnp.exp(-x))


def _log_sigmoid(x):
    return jnp.minimum(x, 0.0) - jnp.log1p(jnp.exp(-jnp.abs(x)))


def _const_spec(shape):
    return pl.BlockSpec(shape, lambda *_: (0,) * len(shape), pipeline_mode=pl.Buffered(1))


def _row_spec(rows, cols):
    return pl.BlockSpec((rows, cols), lambda i: (i, 0))


def _rope_kernel(pos_ref, inv_ref, c_ref, s1_ref, s2_ref):
    ang = pos_ref[...].astype(F32) * inv_ref[...]
    lane = lax.broadcasted_iota(jnp.int32, ang.shape, 1)
    first = (lane >= ROPE_LO) & (lane < ROPE_LO + ROPE_HALF)
    second = (lane >= ROPE_LO + ROPE_HALF) & (lane < ROPE_LO + MLA_ROPE)
    cos, sin = jnp.cos(ang), jnp.sin(ang)
    c_ref[...] = jnp.where(lane < ROPE_LO, 1.0, jnp.where(first | second, cos, 0.0))
    s1_ref[...] = jnp.where(second, sin, 0.0)
    s2_ref[...] = jnp.where(first, -sin, 0.0)


def _rope_tables(positions, tm):
    t = positions.size
    inv = ROPE_BASE ** (-jnp.arange(0, MLA_ROPE, 2, dtype=F32) / MLA_ROPE)
    inv_lane = jnp.zeros((1, LANES), F32)
    inv_lane = inv_lane.at[0, ROPE_LO:ROPE_LO + ROPE_HALF].set(inv)
    inv_lane = inv_lane.at[0, ROPE_LO + ROPE_HALF:ROPE_LO + MLA_ROPE].set(inv)
    out = jax.ShapeDtypeStruct((t, LANES), F32)
    return pl.pallas_call(
        _rope_kernel, grid=(t // tm,),
        in_specs=[_row_spec(tm, 1), _const_spec((1, LANES))],
        out_specs=[_row_spec(tm, LANES)] * 3, out_shape=[out] * 3,
        compiler_params=_params("parallel"), name="rope_tables",
    )(positions.reshape(t, 1), inv_lane)


def _rope(x, c, s1, s2):
    return (x * c + pltpu.roll(x, ROPE_HALF, 1) * s1
            + pltpu.roll(x, LANES - ROPE_HALF, 1) * s2)


def _inproj_kernel(*refs, pre_norm):
    refs = list(refs)
    x_ref = refs.pop(0)
    if pre_norm:
        lng_ref, lnb_ref = refs.pop(0), refs.pop(0)
    (w_ref, wg_ref, bg_ref, gq_ref, wuq_ref, gkv_ref, wuk_ref, wuv_ref, c_ref, s1_ref, s2_ref) = refs[:11]
    outs = refs[11:]
    if pre_norm:
        hn_ref = outs.pop(0)
        h = _layer_norm(x_ref[...], lng_ref[...], lnb_ref[...])
        hn_ref[...] = h
    else:
        h = x_ref[...]
    (ab_ref, ap_ref, du_ref, q_ref, k_ref, v_ref, gg_ref, laf_ref, lab_ref, qc_ref, kc_ref, vc_ref) = outs
    hb = h.astype(BF16)

    def proj(lo, n):
        return _dot(hb, w_ref[:, lo:lo + n])

    ab_ref[...] = proj(COL_SC_B, W_GRP)
    ap_ref[...] = proj(COL_SC_C, W_GRP) * proj(COL_SC_H, W_GRP)
    du_ref[...] = proj(COL_CFM_A, W_GRP) * _sigmoid(proj(COL_CFM_G, W_GRP))
    qk = proj(COL_GQK, 2 * LANES)
    q_ref[...] = qk[:, :LANES] * (GLA_DK ** -0.5)
    k_ref[...] = qk[:, LANES:]
    v_ref[...] = proj(COL_GV, W_GRP)
    gg_ref[...] = proj(COL_GG, W_GRP)
    misc_ckv = proj(COL_MISC_CKV, 2 * LANES)
    misc = misc_ckv[:, :LANES]
    pre = _dot(misc.astype(BF16), wg_ref[...]) + bg_ref[...]
    la = _log_sigmoid(pre) * (1.0 / GLA_TAU)
    laf_ref[...] = la[:, :LANES]
    lab_ref[...] = la[:, LANES:]
    c, s1, s2 = c_ref[...], s1_ref[...], s2_ref[...]
    cqn = _rms_norm(proj(COL_CQ, MLA_Q_RANK), gq_ref[...]).astype(BF16)
    qm = _dot(cqn, wuq_ref[...])
    qscale = (MLA_NOPE + MLA_ROPE) ** -0.5 * LOG2E
    ckvn = _rms_norm(misc_ckv[:, LANES:], gkv_ref[...]).astype(BF16)
    kn = _dot(ckvn, wuk_ref[...])
    vn = _dot(ckvn, wuv_ref[...])
    lane = lax.broadcasted_iota(jnp.int32, misc.shape, 1)
    kr = _rope(jnp.where(lane >= ROPE_LO, misc, 0.0), c, s1, s2)
    ones_col = jnp.where(lane == MLA_V_DIM, 1.0, 0.0)
    for hh in range(MLA_HEADS):
        sl = slice(hh * HEAD_BLOCK, (hh + 1) * HEAD_BLOCK)
        qc_ref[:, sl] = (_rope(qm[:, sl], c, s1, s2) * qscale).astype(BF16)
        kc_ref[:, sl] = (kn[:, sl] + kr).astype(BF16)
        vc_ref[:, sl] = (vn[:, sl] + ones_col).astype(BF16)


def _inproj(h, lw, rope, tm, input_norm=None):
    t = h.shape[0]
    f = lambda n, dt=F32: jax.ShapeDtypeStruct((t, n), dt)
    hb4 = MLA_HEADS * HEAD_BLOCK
    out_shape = [f(W_GRP), f(W_GRP), f(W_GRP), f(LANES), f(LANES), f(W_GRP), f(W_GRP), f(LANES), f(LANES),
                 f(hb4, BF16), f(hb4, BF16), f(hb4, BF16)]
    in_specs = [_const_spec((D_MODEL, D_IN_PAD)),
                _const_spec((LANES, 2 * LANES)), _const_spec((1, 2 * LANES)),
                _const_spec((1, MLA_Q_RANK)), _const_spec((MLA_Q_RANK, hb4)),
                _const_spec((1, MLA_KV_RANK)), _const_spec((MLA_KV_RANK, hb4)), _const_spec((MLA_KV_RANK, hb4)),
                _row_spec(tm, LANES), _row_spec(tm, LANES), _row_spec(tm, LANES)]
    args = [lw["w_in"], lw["w_gate"], lw["b_gate"], lw["g_mla_q"], lw["w_uq"], lw["g_mla_kv"], lw["w_uk"],
            lw["w_uv"], *rope]
    pre_norm = input_norm is not None
    if pre_norm:
        out_shape.insert(0, f(D_MODEL))
        in_specs = [_const_spec((1, D_MODEL)), _const_spec((1, D_MODEL))] + in_specs
        args = [a.reshape(1, D_MODEL) for a in input_norm] + args
    out_specs = [_row_spec(tm, s.shape[1]) for s in out_shape]
    return pl.pallas_call(
        functools.partial(_inproj_kernel, pre_norm=pre_norm), grid=(t // tm,),
        in_specs=[_row_spec(tm, D_MODEL)] + in_specs, out_specs=out_specs, out_shape=out_shape,
        compiler_params=_params("parallel"), name="inproj",
    )(h, *args)


def _gla_kernel(q_ref, k_ref, v_ref, laf_ref, lab_ref, gg_ref, ghead_ref, gain_ref, out_ref,
                acc_ref, sf_ref, sb_ref, *, seq):
    cs = GLA_CHUNK
    grp = min(GLA_GROUP, seq // cs)
    gr = grp * cs
    n_groups = seq // gr
    hk = GLA_HEADS * GLA_DK
    hv = GLA_HEADS * GLA_DV
    row = lax.broadcasted_iota(jnp.int32, (gr, gr), 0)
    col = lax.broadcasted_iota(jnp.int32, (gr, gr), 1)
    tri = jnp.where((row >= col) & (row // cs == col // cs), 1.0, 0.0).astype(BF16)
    srow = lax.broadcasted_iota(jnp.int32, (GLA_HEADS * cs, cs), 0) % cs
    scol = lax.broadcasted_iota(jnp.int32, (GLA_HEADS * cs, cs), 1)
    keep = (srow >= scol, srow < scol)
    qlane_head = lax.broadcasted_iota(jnp.int32, (cs, hk), 1) // GLA_DK
    olane_head = lax.broadcasted_iota(jnp.int32, (cs, hv), 1) // GLA_DV
    srow_head = lax.broadcasted_iota(jnp.int32, (hk, hv), 0) // GLA_DK
    scol_head = lax.broadcasted_iota(jnp.int32, (hk, hv), 1) // GLA_DV
    eye = (lax.broadcasted_iota(jnp.int32, (hk, hk), 0) == lax.broadcasted_iota(jnp.int32, (hk, hk), 1))

    acc_ref[...] = jnp.zeros_like(acc_ref)
    sf_ref[...] = jnp.zeros_like(sf_ref)
    sb_ref[...] = jnp.zeros_like(sb_ref)

    def group_step(g, direction):
        la_ref, s_ref = ((laf_ref, sf_ref), (lab_ref, sb_ref))[direction]
        base = pl.multiple_of(g * gr, gr)
        rows = pl.ds(base, gr)
        la = la_ref[rows, :]
        hi = la.astype(BF16)
        rem = la - hi.astype(F32)
        mid = rem.astype(BF16)
        lo = (rem - mid.astype(F32)).astype(BF16)
        cum = _dot(tri, hi) + _dot(tri, mid) + _dot(tri, lo)
        q, k = q_ref[rows, :], k_ref[rows, :]
        vb = v_ref[rows, :].astype(BF16)
        p, qin, vbs, kv, decay = [], [], [], [], []
        for i in range(grp):
            sl = slice(i * cs, (i + 1) * cs)
            total = cum[(i + 1) * cs - 1:(i + 1) * cs, :]
            b = cum[sl] if direction == 0 else total - cum[sl] + la[sl]
            ref = b[GLA_REF_ROW:GLA_REF_ROW + 1, :]
            qt = q[sl] * jnp.exp(b - ref)
            kt = k[sl] * jnp.exp(ref - b)
            qbd = jnp.concatenate([jnp.where(qlane_head == hh, qt, 0.0) for hh in range(GLA_HEADS)], axis=0)
            scores = _dot_nt(qbd.astype(BF16), kt.astype(BF16))
            p.append(jnp.where(keep[direction], scores, 0.0).astype(BF16))
            qin.append((qbd * jnp.exp(ref)).astype(BF16))
            khat = (kt * jnp.exp(total - ref)).astype(BF16)
            vbs.append(vb[sl])
            kv.append(jnp.where(srow_head == scol_head, _dot_tn(khat, vb[sl]), 0.0))
            decay.append(jnp.sum(jnp.where(eye, jnp.broadcast_to(jnp.exp(total), (hk, hk)), 0.0),
                                 axis=1, keepdims=True))
        state = s_ref[...]
        for i in (range(grp) if direction == 0 else reversed(range(grp))):
            o_all = _dot(p[i], vbs[i]) + _dot(qin[i], state.astype(BF16))
            o = jnp.zeros((cs, hv), F32)
            for hh in range(GLA_HEADS):
                o = o + jnp.where(olane_head == hh, o_all[hh * cs:(hh + 1) * cs, :], 0.0)
            acc_ref[pl.ds(base + i * cs, cs), :] += o
            state = state * decay[i] + kv[i]
        s_ref[...] = state

    def body(n, carry):
        group_step(n, 0)
        group_step(n_groups - 1 - n, 1)
        return carry

    lax.fori_loop(0, n_groups, body, 0)

    hrow = lax.broadcasted_iota(jnp.int32, (hv, hv), 0) // GLA_DV
    hcol = lax.broadcasted_iota(jnp.int32, (hv, hv), 1) // GLA_DV
    head_ones = jnp.where(hrow == hcol, 1.0, 0.0).astype(BF16)
    ft = min(TOKEN_TILE, seq)

    def finish(i, carry):
        rows = pl.ds(pl.multiple_of(i * ft, ft), ft)
        o = acc_ref[rows, :]
        sq = o * o
        hi = sq.astype(BF16)
        lo = (sq - hi.astype(F32)).astype(BF16)
        ms = (_dot(hi, head_ones) + _dot(lo, head_ones)) * (1.0 / GLA_DV)
        on = o * lax.rsqrt(ms + EPS) * ghead_ref[...]
        gg = gg_ref[rows, :]
        yb = on * (gg * _sigmoid(gg))
        out_ref[rows, :] = _rms_norm(yb, gain_ref[...]).astype(BF16)
        return carry

    lax.fori_loop(0, seq // ft, finish, 0)


def _gla(q, k, v, laf, lab, gg, ghead, gain, bsz, seq):
    hk, hv = GLA_HEADS * GLA_DK, GLA_HEADS * GLA_DV
    seq_spec = lambda n: pl.BlockSpec((seq, n), lambda b: (b, 0))
    return pl.pallas_call(
        functools.partial(_gla_kernel, seq=seq), grid=(bsz,),
        in_specs=[seq_spec(hk), seq_spec(hk), seq_spec(hv), seq_spec(hk), seq_spec(hk), seq_spec(hv),
                  _const_spec((1, hv)), _const_spec((1, hv))],
        out_specs=seq_spec(hv), out_shape=jax.ShapeDtypeStruct((bsz * seq, hv), BF16),
        scratch_shapes=[pltpu.VMEM((seq, hv), F32), pltpu.VMEM((hk, hv), F32), pltpu.VMEM((hk, hv), F32)],
        compiler_params=_params("parallel"), name="gla",
    )(q, k, v, laf, lab, gg, ghead, gain)


def _attn_kernel(q_ref, k_ref, v_ref, gain_ref, out_ref, s0_ref, s1_ref, *, seq, tk):
    n_tiles = seq // tk
    bufs = (s0_ref, s1_ref)
    tiles = [(hh, j) for hh in range(MLA_HEADS) for j in range(n_tiles)]

    def scores(dst_ref, hh, j):
        sl = slice(hh * HEAD_BLOCK, (hh + 1) * HEAD_BLOCK)
        dst_ref[...] = _dot_nt(q_ref[:, sl], k_ref[j * tk:(j + 1) * tk, sl])

    heads = []
    m = acc = None
    scores(bufs[0], *tiles[0])
    for idx, (hh, j) in enumerate(tiles):
        if idx + 1 < len(tiles):
            scores(bufs[(idx + 1) % 2], *tiles[idx + 1])
        s = bufs[idx % 2][...]
        pv = lambda p: _dot(p.astype(BF16), v_ref[j * tk:(j + 1) * tk, hh * HEAD_BLOCK:(hh + 1) * HEAD_BLOCK])
        m_tile = jnp.max(s, axis=1, keepdims=True)
        if j == 0:
            m = m_tile
            acc = pv(jnp.exp2(s - m))
        else:
            m_new = jnp.maximum(m, m_tile)
            acc = acc * jnp.exp2(m - m_new) + pv(jnp.exp2(s - m_new))
            m = m_new
        if j == n_tiles - 1:
            heads.append(acc[:, :MLA_V_DIM] / acc[:, MLA_V_DIM:MLA_V_DIM + 1])
    y = jnp.concatenate(heads, axis=1)
    out_ref[...] = _rms_norm(y, gain_ref[...]).astype(BF16)


def _attention(qc, kc, vc, gain, bsz, seq):
    tq, tk = min(ATTN_TQ, seq), min(ATTN_TK, seq)
    nq = seq // tq
    hb4 = MLA_HEADS * HEAD_BLOCK
    kv_spec = pl.BlockSpec((seq, hb4), lambda b, i: (b, 0))
    return pl.pallas_call(
        functools.partial(_attn_kernel, seq=seq, tk=tk), grid=(bsz, nq),
        in_specs=[pl.BlockSpec((tq, hb4), lambda b, i: (b * nq + i, 0)), kv_spec, kv_spec,
                  _const_spec((1, W_GRP))],
        out_specs=pl.BlockSpec((tq, W_GRP), lambda b, i: (b * nq + i, 0)),
        out_shape=jax.ShapeDtypeStruct((bsz * seq, W_GRP), BF16),
        scratch_shapes=[pltpu.VMEM((tq, tk), F32), pltpu.VMEM((tq, tk), F32)],
        compiler_params=_params("parallel", "parallel"), name="mla_attention",
    )(qc, kc, vc, gain)


def _outproj_kernel(ab_ref, ap_ref, app_ref, apn_ref, du_ref, dup_ref, dun_ref, yb_ref, yc_ref, h_ref,
                    wsc_ref, wcfm_ref, gcfm_ref, bcfm_ref, gain_ref, wout_ref, g1_ref, b1_ref,
                    out_ref, pext_ref, uext_ref, shift_ref, *, tiles_per_seq):
    tm = ab_ref.shape[0]
    i = pl.program_id(0) % tiles_per_seq
    not_first = (i != 0).astype(F32)
    not_last = (i != tiles_per_seq - 1).astype(F32)
    pext_ref[0:HALO, :] = app_ref[...] * not_first
    pext_ref[HALO:HALO + tm, :] = ap_ref[...]
    pext_ref[HALO + tm:, :] = apn_ref[...] * not_last
    uext_ref[0:HALO, :] = dup_ref[...] * not_first
    uext_ref[HALO:HALO + tm, :] = du_ref[...]
    uext_ref[HALO + tm:, :] = dun_ref[...] * not_last

    conv = jnp.zeros((tm, W_GRP), F32)
    for j in range(SC_WIDTH):
        conv = conv + pext_ref[pl.ds(HALO - SC_WIDTH // 2 + j, tm), :] * wsc_ref[j:j + 1, :]
    ya = _rms_norm(ab_ref[...] * conv, gain_ref[:, 0:W_GRP])
    first = HALO - CFM_WIDTH // 2
    span = SUBLANES * ((first + CFM_WIDTH - 1) // SUBLANES)
    assert shift_ref.shape[0] == tm + span and span + SUBLANES <= 2 * HALO
    u = jnp.zeros((tm, W_GRP), F32)
    for r in range(SUBLANES):
        shift_ref[...] = uext_ref[pl.ds(r, tm + span), :]
        for off in range(0, span + 1, SUBLANES):
            j = off + r - first
            if 0 <= j < CFM_WIDTH:
                u = u + shift_ref[pl.ds(off, tm), :] * wcfm_ref[j:j + 1, :]
    un = _layer_norm(u, gcfm_ref[...], bcfm_ref[...])
    yd = _rms_norm(un * _sigmoid(un), gain_ref[:, 3 * W_GRP:4 * W_GRP])

    mix = (_dot(ya.astype(BF16), wout_ref[0:W_GRP, :])
           + _dot(yb_ref[...], wout_ref[W_GRP:2 * W_GRP, :])
           + _dot(yc_ref[...], wout_ref[2 * W_GRP:3 * W_GRP, :])
           + _dot(yd.astype(BF16), wout_ref[3 * W_GRP:4 * W_GRP, :]))
    out_ref[...] = _layer_norm(DN_ALPHA * h_ref[...] + mix, g1_ref[...], b1_ref[...])


def _halo_specs(tm, cols, halo, total_rows):
    per = tm // halo
    last = total_rows // halo - 1
    prev = pl.BlockSpec((halo, cols), lambda i: (jnp.maximum(i * per - 1, 0), 0))
    nxt = pl.BlockSpec((halo, cols), lambda i: (jnp.minimum((i + 1) * per, last), 0))
    return prev, nxt


def _outproj(ab, ap, du, yb, yc, h, lw, seq, tm):
    t = h.shape[0]
    prev, nxt = _halo_specs(tm, W_GRP, HALO, t)
    g = _row_spec(tm, W_GRP)
    in_specs = [g, g, prev, nxt, g, prev, nxt, g, g, _row_spec(tm, D_MODEL),
                _const_spec((SC_WIDTH, W_GRP)), _const_spec((CFM_WIDTH, W_GRP)),
                _const_spec((1, W_GRP)), _const_spec((1, W_GRP)), _const_spec((1, 4 * W_GRP)),
                _const_spec((4 * W_GRP, D_MODEL)), _const_spec((1, D_MODEL)), _const_spec((1, D_MODEL))]
    return pl.pallas_call(
        functools.partial(_outproj_kernel, tiles_per_seq=seq // tm), grid=(t // tm,),
        in_specs=in_specs, out_specs=_row_spec(tm, D_MODEL),
        out_shape=jax.ShapeDtypeStruct((t, D_MODEL), F32),
        scratch_shapes=[pltpu.VMEM((tm + 2 * HALO, W_GRP), F32), pltpu.VMEM((tm + 2 * HALO, W_GRP), F32),
                        pltpu.VMEM((tm + 2 * HALO - SUBLANES, W_GRP), F32)],
        compiler_params=_params("parallel"), name="outproj",
    )(ab, ap, ap, ap, du, du, du, yb, yc, h, lw["w_sc_conv"], lw["w_cfm_dw"], lw["g_cfm_ln"],
      lw["b_cfm_ln"], lw["g_branch"], lw["w_out"], lw["ln1_g"], lw["ln1_b"])


def _ffn_kernel(h_ref, hp_ref, hn_ref, wup_ref, wdw_ref, wdown_ref, g2_ref, b2_ref, out_ref,
                hx_ref, acc_ref, ug0_ref, uv0_ref, ug1_ref, uv1_ref, *, tiles_per_seq, n_chunks):
    tm = h_ref.shape[0]
    i = pl.program_id(0) % tiles_per_seq
    not_first = (i != 0).astype(F32)
    not_last = (i != tiles_per_seq - 1).astype(F32)
    ext = tm + 2 * FFN_HALO
    hx_ref[0:FFN_HALO, :] = (hp_ref[...] * not_first).astype(BF16)
    hx_ref[FFN_HALO:FFN_HALO + tm, :] = h_ref[...].astype(BF16)
    hx_ref[FFN_HALO + tm:, :] = (hn_ref[...] * not_last).astype(BF16)
    acc_ref[...] = jnp.zeros_like(acc_ref)

    def conv3(u, w):
        prev = pltpu.roll(u, 1, 0)
        nxt = pltpu.roll(u, ext - 1, 0)
        full = prev * w[0:1, :] + u * w[1:2, :] + nxt * w[2:3, :]
        return full[FFN_HALO:FFN_HALO + tm, :]

    bufs = ((ug0_ref, uv0_ref), (ug1_ref, uv1_ref))

    def up_cols(c, half):
        return wup_ref[:, pl.ds(half * D_FF + c * FF_CHUNK, FF_CHUNK)]

    def up(slot, c):
        hx = hx_ref[...]
        bufs[slot][0][...] = _dot(hx, up_cols(c, 0))
        bufs[slot][1][...] = _dot(hx, up_cols(c, 1))

    def activation(slot, c):
        gate = conv3(bufs[slot][0][...], wdw_ref[0, c])
        val = conv3(bufs[slot][1][...], wdw_ref[1, c])
        return (gate * _sigmoid(gate) * val).astype(BF16)

    def down_rows(c, n):
        return wdown_ref[pl.ds(c * FF_CHUNK, n * FF_CHUNK), :]

    def body(jj, carry):
        c = 2 * jj
        up(1, c + 1)
        a0 = activation(0, c)
        up(0, c + 2)
        a1 = activation(1, c + 1)
        acc_ref[...] += _dot(jnp.concatenate([a0, a1], axis=1), down_rows(c, 2))
        return carry

    assert n_chunks % 2 == 1
    up(0, 0)
    for jj in range(n_chunks // 2):
        body(jj, 0)
    last = n_chunks - 1
    mix = acc_ref[...] + _dot(activation(0, last), down_rows(last, 1))
    out_ref[...] = _layer_norm(DN_ALPHA * h_ref[...] + mix, g2_ref[...], b2_ref[...])


def _ffn(h, lw, seq, tm):
    t = h.shape[0]
    n_chunks = D_FF // FF_CHUNK
    prev, nxt = _halo_specs(tm, D_MODEL, FFN_HALO, t)
    in_specs = [_row_spec(tm, D_MODEL), prev, nxt,
                _const_spec((D_MODEL, 2 * D_FF)), _const_spec((2, n_chunks, FFN_CONV_WIDTH, FF_CHUNK)),
                _const_spec((D_FF, D_MODEL)), _const_spec((1, D_MODEL)), _const_spec((1, D_MODEL))]
    ext = tm + 2 * FFN_HALO
    return pl.pallas_call(
        functools.partial(_ffn_kernel, tiles_per_seq=seq // tm, n_chunks=n_chunks), grid=(t // tm,),
        in_specs=in_specs, out_specs=_row_spec(tm, D_MODEL),
        out_shape=jax.ShapeDtypeStruct((t, D_MODEL), F32),
        scratch_shapes=[pltpu.VMEM((ext, D_MODEL), BF16), pltpu.VMEM((tm, D_MODEL), F32)]
        + [pltpu.VMEM((ext, FF_CHUNK), F32)] * 4,
        compiler_params=_params("parallel"), name="conv_ffn",
    )(h, h, h, lw["w_ffn_up"], lw["w_ffn_dw"], lw["w_ffn_down"], lw["ln2_g"], lw["ln2_b"])


def _head_blocks(w, per_head, lo, n):
    rows = w.shape[0]
    wh = w.reshape(rows, -1, per_head)[:, :, lo:lo + n]
    return jnp.pad(wh, ((0, 0), (0, 0), (0, HEAD_BLOCK - n))).reshape(rows, -1)


def _layer_weights(p, l):
    w_in = p["w_in"][l]
    sizes = (256, 256, 256, 128, 128, 256, 256, 16, 16, 256, 128, 32, 512)
    offs = [0]
    for s in sizes:
        offs.append(offs[-1] + s)
    seg = lambda idx: w_in[:, offs[idx]:offs[idx + 1]]
    zeros = lambda n: jnp.zeros((D_MODEL, n), F32)
    misc = jnp.concatenate([seg(7), seg(8), zeros(ROPE_LO - 2 * GLA_LOWRANK), seg(11),
                            zeros(LANES - ROPE_LO - MLA_ROPE)], axis=1)
    w_in_p = jnp.concatenate([seg(0), seg(1), seg(2), seg(3), seg(4), seg(5), seg(6), misc, seg(10), seg(9),
                              seg(12)], axis=1).astype(BF16)
    hk = GLA_HEADS * GLA_DK
    w_gate = jnp.zeros((LANES, 2 * hk), F32)
    w_gate = w_gate.at[0:GLA_LOWRANK, 0:hk].set(p["w_gla_a_up"][l, 0])
    w_gate = w_gate.at[GLA_LOWRANK:2 * GLA_LOWRANK, hk:].set(p["w_gla_a_up"][l, 1])
    w_uq = p["w_mla_uq"][l]
    w_ukv = p["w_mla_ukv"][l]
    n_ff = D_FF // FF_CHUNK
    w_dw = p["w_ffn_dw"][l].reshape(FFN_CONV_WIDTH, 2, n_ff, FF_CHUNK).transpose(1, 2, 0, 3)
    row = lambda a: a.reshape(1, -1)
    return {
        "w_in": w_in_p,
        "w_gate": w_gate.astype(BF16),
        "b_gate": row(p["b_gla_a"][l]),
        "g_mla_q": row(p["g_mla_q"][l]),
        "w_uq": _head_blocks(w_uq, MLA_NOPE + MLA_ROPE, 0, MLA_NOPE + MLA_ROPE).astype(BF16),
        "g_mla_kv": row(p["g_mla_kv"][l]),
        "w_uk": _head_blocks(w_ukv, MLA_NOPE + MLA_V_DIM, 0, MLA_NOPE).astype(BF16),
        "w_uv": _head_blocks(w_ukv, MLA_NOPE + MLA_V_DIM, MLA_NOPE, MLA_V_DIM).astype(BF16),
        "g_gla_head": row(p["g_gla_head"][l]),
        "w_sc_conv": p["w_sc_conv"][l],
        "w_cfm_dw": p["w_cfm_dw"][l],
        "g_cfm_ln": row(p["g_cfm_ln"][l]),
        "b_cfm_ln": row(p["b_cfm_ln"][l]),
        "g_branch": row(p["g_branch"][l]),
        "w_out": p["w_out"][l].astype(BF16),
        "ln1_g": row(p["ln1_g"][l]),
        "ln1_b": row(p["ln1_b"][l]),
        "w_ffn_up": p["w_ffn_up"][l].astype(BF16),
        "w_ffn_dw": w_dw,
        "w_ffn_down": p["w_ffn_down"][l].astype(BF16),
        "ln2_g": row(p["ln2_g"][l]),
        "ln2_b": row(p["ln2_b"][l]),
    }


def kernel(x, positions, ln_in_g, ln_in_b, w_in, w_sc_conv, w_gla_a_up, b_gla_a, g_gla_head, g_mla_q, w_mla_uq,
           g_mla_kv, w_mla_ukv, w_cfm_dw, g_cfm_ln, b_cfm_ln, g_branch, w_out, ln1_g, ln1_b, w_ffn_up, w_ffn_dw,
           w_ffn_down, ln2_g, ln2_b):
    p = dict(w_in=w_in, w_sc_conv=w_sc_conv, w_gla_a_up=w_gla_a_up, b_gla_a=b_gla_a, g_gla_head=g_gla_head,
             g_mla_q=g_mla_q, w_mla_uq=w_mla_uq, g_mla_kv=g_mla_kv, w_mla_ukv=w_mla_ukv, w_cfm_dw=w_cfm_dw,
             g_cfm_ln=g_cfm_ln, b_cfm_ln=b_cfm_ln, g_branch=g_branch, w_out=w_out, ln1_g=ln1_g, ln1_b=ln1_b,
             w_ffn_up=w_ffn_up, w_ffn_dw=w_ffn_dw, w_ffn_down=w_ffn_down, ln2_g=ln2_g, ln2_b=ln2_b)
    bsz, seq, d = x.shape
    assert d == D_MODEL and seq % GLA_CHUNK == 0
    tm = min(TOKEN_TILE, seq)
    assert seq % tm == 0 and tm % HALO == 0
    t = bsz * seq
    rope = _rope_tables(positions, tm)
    h = x.reshape(t, d)
    for l in range(DEPTH):
        lw = _layer_weights(p, l)
        if l == 0:
            h, *mixer_in = _inproj(h, lw, rope, tm, input_norm=(ln_in_g, ln_in_b))
        else:
            mixer_in = _inproj(h, lw, rope, tm)
        ab, ap, du, q, k, v, gg, laf, lab, qc, kc, vc = mixer_in
        gain = lw["g_branch"]
        yb = _gla(q, k, v, laf, lab, gg, lw["g_gla_head"], gain[:, W_GRP:2 * W_GRP], bsz, seq)
        yc = _attention(qc, kc, vc, gain[:, 2 * W_GRP:3 * W_GRP], bsz, seq)
        h = _outproj(ab, ap, du, yb, yc, h, lw, seq, tm)
        h = _ffn(h, lw, seq, min(FFN_TILE, seq))
    return h.reshape(bsz, seq, d)
```

```python
import functools
import math

import jax
import jax.numpy as jnp
from jax import lax
from jax.experimental import pallas as pl
from jax.experimental.pallas import tpu as pltpu

F32 = jnp.float32
BF16 = jnp.bfloat16

D_MODEL = 1024
DEPTH = 2
W_GRP = 256
GROUP_A, GROUP_B, GROUP_C, GROUP_D = range(4)
SC_WIDTH = 3
GLA_HEADS = 4
GLA_DK = 32
GLA_DV = 64
GLA_LOWRANK = 16
GLA_TAU = 16.0
MLA_HEADS = 4
MLA_V_DIM = 64
MLA_NOPE = 64
MLA_ROPE = 32
MLA_Q_RANK = 256
MLA_KV_RANK = 128
ROPE_BASE = 10000.0
CFM_WIDTH = 31
D_FF = 2816
FFN_CONV_WIDTH = 3
DN_ALPHA = (2.0 * DEPTH) ** 0.25
EPS = 1e-5

LANES = 128
SUBLANES = 8
VMEM_LIMIT_BYTES = 56 * 1024 * 1024

TOKEN_TILE = 512
FFN_TILE = 1024
HALO = 16
FFN_HALO = SUBLANES
FF_CHUNK = 256
GLA_CHUNK = 64
GLA_REF_ROW = GLA_CHUNK // 2
GLA_GROUP = 4
ATTN_TQ = 512
ATTN_TK = 2048
HEAD_BLOCK = LANES
ROPE_LO = MLA_NOPE
ROPE_HALF = MLA_ROPE // 2
LOG2E = math.log2(math.e)

COL_SC_B, COL_SC_C, COL_SC_H = 0, 256, 512
COL_GQK, COL_GV, COL_GG = 768, 1024, 1280
COL_MISC_CKV = 1536
COL_CQ = 1792
COL_CFM_A, COL_CFM_G = 2048, 2304
D_IN_PAD = 2560


def _params(*sem):
    return pltpu.CompilerParams(dimension_semantics=sem, vmem_limit_bytes=VMEM_LIMIT_BYTES)


def _dot(a, b):
    return jnp.dot(a, b, preferred_element_type=F32)


def _dot_nt(a, b):
    return lax.dot_general(a, b, (((1,), (1,)), ((), ())), preferred_element_type=F32)


def _dot_tn(a, b):
    return lax.dot_general(a, b, (((0,), (0,)), ((), ())), preferred_element_type=F32)


def _layer_norm(x, g, b):
    mu = jnp.mean(x, axis=-1, keepdims=True)
    xc = x - mu
    var = jnp.mean(xc * xc, axis=-1, keepdims=True)
    return xc * lax.rsqrt(var + EPS) * g + b


def _rms_norm(x, g):
    return x * lax.rsqrt(jnp.mean(x * x, axis=-1, keepdims=True) + EPS) * g


def _sigmoid(x):
    return 1.0 / (1.0 + jnp.exp(-x))


def _log_sigmoid(x):
    return jnp.minimum(x, 0.0) - jnp.log1p(jnp.exp(-jnp.abs(x)))


def _const_spec(shape):
    return pl.BlockSpec(shape, lambda *_: (0,) * len(shape), pipeline_mode=pl.Buffered(1))


def _layer_spec(shape, layer, block=None):
    idx = (layer,) + (tuple(block) if block is not None else (0,) * len(shape))
    return pl.BlockSpec((None,) + tuple(shape), lambda *_: idx, pipeline_mode=pl.Buffered(1))


def _row_spec(rows, cols):
    return pl.BlockSpec((rows, cols), lambda i: (i, 0))


def _rope_kernel(pos_ref, inv_ref, c_ref, s1_ref, s2_ref):
    ang = inv_ref[...] * pos_ref[...].astype(F32)
    frow = lax.broadcasted_iota(jnp.int32, (ROPE_HALF, LANES), 0)
    lane = lax.broadcasted_iota(jnp.int32, (ROPE_HALF, LANES), 1)
    to_first = jnp.where(lane == ROPE_LO + frow, 1.0, 0.0).astype(BF16)
    to_second = jnp.where(lane == ROPE_LO + ROPE_HALF + frow, 1.0, 0.0).astype(BF16)

    def place(x, where_to):
        hi = x.astype(BF16)
        rem = x - hi.astype(F32)
        mid = rem.astype(BF16)
        lo = (rem - mid.astype(F32)).astype(BF16)
        return _dot_tn(hi, where_to) + _dot_tn(mid, where_to) + _dot_tn(lo, where_to)

    cos, sin = jnp.cos(ang), jnp.sin(ang)
    out_lane = lax.broadcasted_iota(jnp.int32, c_ref.shape, 1)
    c_ref[...] = jnp.where(out_lane < ROPE_LO, 1.0, place(cos, to_first + to_second))
    s1_ref[...] = place(sin, to_second)
    s2_ref[...] = -place(sin, to_first)


def _rope_tables(positions, tm):
    t = positions.size
    inv = ROPE_BASE ** (-jnp.arange(0, MLA_ROPE, 2, dtype=F32) / MLA_ROPE)
    out = jax.ShapeDtypeStruct((t, LANES), F32)
    return pl.pallas_call(
        _rope_kernel, grid=(t // tm,),
        in_specs=[pl.BlockSpec((1, tm), lambda i: (0, i)), _const_spec((ROPE_HALF, 1))],
        out_specs=[_row_spec(tm, LANES)] * 3, out_shape=[out] * 3,
        compiler_params=_params("parallel"), name="rope_tables",
    )(positions.reshape(1, t), inv.reshape(ROPE_HALF, 1))


def _rope(x, c, s1, s2):
    return (x * c + pltpu.roll(x, ROPE_HALF, 1) * s1
            + pltpu.roll(x, LANES - ROPE_HALF, 1) * s2)


def _inproj_kernel(*refs, pre_norm):
    refs = list(refs)
    x_ref = refs.pop(0)
    if pre_norm:
        lng_ref, lnb_ref = refs.pop(0), refs.pop(0)
    (w_ref, wg_ref, bg_ref, gq_ref, wuq_ref, gkv_ref, wuk_ref, wuv_ref, c_ref, s1_ref, s2_ref) = refs[:11]
    outs = refs[11:]
    if pre_norm:
        hn_ref = outs.pop(0)
        h = _layer_norm(x_ref[...], lng_ref[...], lnb_ref[...])
        hn_ref[...] = h
    else:
        h = x_ref[...]
    (ab_ref, ap_ref, du_ref, q_ref, k_ref, v_ref, gg_ref, laf_ref, lab_ref, qc_ref, kc_ref, vc_ref) = outs
    hb = h.astype(BF16)

    def proj(lo, n):
        return _dot(hb, w_ref[:, lo:lo + n])

    misc_ckv = proj(COL_MISC_CKV, 2 * LANES)
    cq = proj(COL_CQ, MLA_Q_RANK)
    misc = misc_ckv[:, :LANES]
    pre = _dot(misc.astype(BF16), wg_ref[...]) + bg_ref[...]
    la = _log_sigmoid(pre) * (1.0 / GLA_TAU)
    laf_ref[...] = la[:, :LANES]
    lab_ref[...] = la[:, LANES:]
    c, s1, s2 = c_ref[...], s1_ref[...], s2_ref[...]
    cqn = _rms_norm(cq, gq_ref[...]).astype(BF16)
    qm = _dot(cqn, wuq_ref[...])
    qscale = (MLA_NOPE + MLA_ROPE) ** -0.5 * LOG2E
    ckvn = _rms_norm(misc_ckv[:, LANES:], gkv_ref[...]).astype(BF16)
    kn = _dot(ckvn, wuk_ref[...])
    vn = _dot(ckvn, wuv_ref[...])
    lane = lax.broadcasted_iota(jnp.int32, misc.shape, 1)
    kr = _rope(jnp.where(lane >= ROPE_LO, misc, 0.0), c, s1, s2)
    ones_col = jnp.where(lane == MLA_V_DIM, 1.0, 0.0)
    for hh in range(MLA_HEADS):
        sl = slice(hh * HEAD_BLOCK, (hh + 1) * HEAD_BLOCK)
        qc_ref[:, sl] = (_rope(qm[:, sl], c, s1, s2) * qscale).astype(BF16)
        kc_ref[:, sl] = (kn[:, sl] + kr).astype(BF16)
        vc_ref[:, sl] = (vn[:, sl] + ones_col).astype(BF16)
    ab_ref[...] = proj(COL_SC_B, W_GRP)
    ap_ref[...] = proj(COL_SC_C, W_GRP) * proj(COL_SC_H, W_GRP)
    du_ref[...] = proj(COL_CFM_A, W_GRP) * _sigmoid(proj(COL_CFM_G, W_GRP))
    qk = proj(COL_GQK, 2 * LANES)
    q_ref[...] = qk[:, :LANES] * (GLA_DK ** -0.5)
    k_ref[...] = qk[:, LANES:]
    v_ref[...] = proj(COL_GV, W_GRP)
    gg_ref[...] = proj(COL_GG, W_GRP)


def _inproj(h, lw, layer, rope, tm, input_norm=None):
    t = h.shape[0]
    f = lambda n, dt=F32: jax.ShapeDtypeStruct((t, n), dt)
    hb4 = MLA_HEADS * HEAD_BLOCK
    out_shape = [f(W_GRP), f(W_GRP), f(W_GRP), f(LANES), f(LANES), f(W_GRP), f(W_GRP), f(LANES), f(LANES),
                 f(hb4, BF16), f(hb4, BF16), f(hb4, BF16)]
    ls = functools.partial(_layer_spec, layer=layer)
    in_specs = [ls((D_MODEL, D_IN_PAD)), ls((LANES, 2 * LANES)), ls((1, 2 * LANES)),
                ls((1, MLA_Q_RANK)), ls((MLA_Q_RANK, hb4)),
                ls((1, MLA_KV_RANK)), ls((MLA_KV_RANK, hb4)), ls((MLA_KV_RANK, hb4)),
                _row_spec(tm, LANES), _row_spec(tm, LANES), _row_spec(tm, LANES)]
    args = [lw["w_in"], lw["w_gate"], lw["b_gate"], lw["g_mla_q"], lw["w_uq"], lw["g_mla_kv"], lw["w_uk"],
            lw["w_uv"], *rope]
    pre_norm = input_norm is not None
    if pre_norm:
        out_shape.insert(0, f(D_MODEL))
        in_specs = [_const_spec((1, D_MODEL)), _const_spec((1, D_MODEL))] + in_specs
        args = [a.reshape(1, D_MODEL) for a in input_norm] + args
    out_specs = [_row_spec(tm, s.shape[1]) for s in out_shape]
    return pl.pallas_call(
        functools.partial(_inproj_kernel, pre_norm=pre_norm), grid=(t // tm,),
        in_specs=[_row_spec(tm, D_MODEL)] + in_specs, out_specs=out_specs, out_shape=out_shape,
        compiler_params=_params("parallel"), name="inproj",
    )(h, *args)


def _gla_kernel(q_ref, k_ref, v_ref, laf_ref, lab_ref, gg_ref, ghead_ref, gain_ref, out_ref,
                acc_ref, sf_ref, sb_ref, *, seq):
    cs = GLA_CHUNK
    grp = min(GLA_GROUP, seq // cs)
    gr = grp * cs
    n_groups = seq // gr
    hk = GLA_HEADS * GLA_DK
    hv = GLA_HEADS * GLA_DV
    row = lax.broadcasted_iota(jnp.int32, (gr, gr), 0)
    col = lax.broadcasted_iota(jnp.int32, (gr, gr), 1)
    tri = jnp.where((row >= col) & (row // cs == col // cs), 1.0, 0.0).astype(BF16)
    srow = lax.broadcasted_iota(jnp.int32, (GLA_HEADS * cs, cs), 0) % cs
    scol = lax.broadcasted_iota(jnp.int32, (GLA_HEADS * cs, cs), 1)
    keep = (srow >= scol, srow < scol)
    qlane_head = lax.broadcasted_iota(jnp.int32, (cs, hk), 1) // GLA_DK
    olane_head = lax.broadcasted_iota(jnp.int32, (cs, hv), 1) // GLA_DV
    srow_head = lax.broadcasted_iota(jnp.int32, (hk, hv), 0) // GLA_DK
    scol_head = lax.broadcasted_iota(jnp.int32, (hk, hv), 1) // GLA_DV
    eye = (lax.broadcasted_iota(jnp.int32, (hk, hk), 0) == lax.broadcasted_iota(jnp.int32, (hk, hk), 1))

    acc_ref[...] = jnp.zeros_like(acc_ref)
    sf_ref[...] = jnp.zeros_like(sf_ref)
    sb_ref[...] = jnp.zeros_like(sb_ref)

    def decay_phase(g, direction):
        la_ref = (laf_ref, lab_ref)[direction]
        base = pl.multiple_of(g * gr, gr)
        la = la_ref[pl.ds(base, gr), :]
        hi = la.astype(BF16)
        rem = la - hi.astype(F32)
        mid = rem.astype(BF16)
        lo = (rem - mid.astype(F32)).astype(BF16)
        cum = _dot(tri, hi) + _dot(tri, mid) + _dot(tri, lo)
        return base, la, cum

    def chunk_phase(direction, base, la, cum):
        rows = pl.ds(base, gr)
        q, k = q_ref[rows, :], k_ref[rows, :]
        vb = v_ref[rows, :].astype(BF16)
        p, qin, vbs, kv, decay = [], [], [], [], []
        for i in range(grp):
            sl = slice(i * cs, (i + 1) * cs)
            total = cum[(i + 1) * cs - 1:(i + 1) * cs, :]
            b = cum[sl] if direction == 0 else total - cum[sl] + la[sl]
            ref = b[GLA_REF_ROW:GLA_REF_ROW + 1, :]
            qt = q[sl] * jnp.exp(b - ref)
            kt = k[sl] * jnp.exp(ref - b)
            qbd = jnp.concatenate([jnp.where(qlane_head == hh, qt, 0.0) for hh in range(GLA_HEADS)], axis=0)
            scores = _dot_nt(qbd.astype(BF16), kt.astype(BF16))
            p.append(jnp.where(keep[direction], scores, 0.0).astype(BF16))
            qin.append((qbd * jnp.exp(ref)).astype(BF16))
            khat = (kt * jnp.exp(total - ref)).astype(BF16)
            vbs.append(vb[sl])
            kv.append(jnp.where(srow_head == scol_head, _dot_tn(khat, vb[sl]), 0.0))
            decay.append(jnp.sum(jnp.where(eye, jnp.broadcast_to(jnp.exp(total), (hk, hk)), 0.0),
                                 axis=1, keepdims=True))
        return p, qin, vbs, kv, decay

    def state_phase(direction, base, p, qin, vbs, kv, decay):
        s_ref = (sf_ref, sb_ref)[direction]
        state = s_ref[...]
        for i in (range(grp) if direction == 0 else reversed(range(grp))):
            lhs = jnp.concatenate([qin[i], p[i]], axis=1)
            rhs = jnp.concatenate([state.astype(BF16), vbs[i]], axis=0)
            o_all = _dot(lhs, rhs)
            o = jnp.zeros((cs, hv), F32)
            for hh in range(GLA_HEADS):
                o = o + jnp.where(olane_head == hh, o_all[hh * cs:(hh + 1) * cs, :], 0.0)
            acc_ref[pl.ds(base + i * cs, cs), :] += o
            state = state * decay[i] + kv[i]
        s_ref[...] = state

    def body(n, carry):
        fwd = decay_phase(n, 0)
        bwd = decay_phase(n_groups - 1 - n, 1)
        fwd_chunks = chunk_phase(0, *fwd)
        bwd_chunks = chunk_phase(1, *bwd)
        state_phase(0, fwd[0], *fwd_chunks)
        state_phase(1, bwd[0], *bwd_chunks)
        return carry

    lax.fori_loop(0, n_groups, body, 0)

    hrow = lax.broadcasted_iota(jnp.int32, (hv, hv), 0) // GLA_DV
    hcol = lax.broadcasted_iota(jnp.int32, (hv, hv), 1) // GLA_DV
    head_ones = jnp.where(hrow == hcol, 1.0, 0.0).astype(BF16)
    ft = min(TOKEN_TILE, seq)

    def finish(i, carry):
        rows = pl.ds(pl.multiple_of(i * ft, ft), ft)
        o = acc_ref[rows, :]
        sq = o * o
        hi = sq.astype(BF16)
        lo = (sq - hi.astype(F32)).astype(BF16)
        ms = (_dot(hi, head_ones) + _dot(lo, head_ones)) * (1.0 / GLA_DV)
        on = o * lax.rsqrt(ms + EPS) * ghead_ref[...]
        gg = gg_ref[rows, :]
        yb = on * (gg * _sigmoid(gg))
        out_ref[rows, :] = _rms_norm(yb, gain_ref[...]).astype(BF16)
        return carry

    lax.fori_loop(0, seq // ft, finish, 0)


def _gla(q, k, v, laf, lab, gg, lw, layer, bsz, seq):
    hk, hv = GLA_HEADS * GLA_DK, GLA_HEADS * GLA_DV
    seq_spec = lambda n: pl.BlockSpec((seq, n), lambda b: (b, 0))
    return pl.pallas_call(
        functools.partial(_gla_kernel, seq=seq), grid=(bsz,),
        in_specs=[seq_spec(hk), seq_spec(hk), seq_spec(hv), seq_spec(hk), seq_spec(hk), seq_spec(hv),
                  _layer_spec((1, hv), layer), _layer_spec((1, W_GRP), layer, block=(0, GROUP_B))],
        out_specs=seq_spec(hv), out_shape=jax.ShapeDtypeStruct((bsz * seq, hv), BF16),
        scratch_shapes=[pltpu.VMEM((seq, hv), F32), pltpu.VMEM((hk, hv), F32), pltpu.VMEM((hk, hv), F32)],
        compiler_params=_params("parallel"), name="gla",
    )(q, k, v, laf, lab, gg, lw["g_gla_head"], lw["g_branch"])


def _attn_kernel(q_ref, k_ref, v_ref, gain_ref, out_ref, s0_ref, s1_ref, *, seq, tk):
    n_tiles = seq // tk
    bufs = (s0_ref, s1_ref)
    tiles = [(hh, j) for hh in range(MLA_HEADS) for j in range(n_tiles)]

    def scores(dst_ref, hh, j):
        sl = slice(hh * HEAD_BLOCK, (hh + 1) * HEAD_BLOCK)
        dst_ref[...] = _dot_nt(q_ref[:, sl], k_ref[j * tk:(j + 1) * tk, sl])

    heads = []
    m = acc = None
    scores(bufs[0], *tiles[0])
    for idx, (hh, j) in enumerate(tiles):
        if idx + 1 < len(tiles):
            scores(bufs[(idx + 1) % 2], *tiles[idx + 1])
        s = bufs[idx % 2][...]
        pv = lambda p: _dot(p.astype(BF16), v_ref[j * tk:(j + 1) * tk, hh * HEAD_BLOCK:(hh + 1) * HEAD_BLOCK])
        m_tile = jnp.max(s, axis=1, keepdims=True)
        if j == 0:
            m = m_tile
            acc = pv(jnp.exp2(s - m))
        else:
            m_new = jnp.maximum(m, m_tile)
            acc = acc * jnp.exp2(m - m_new) + pv(jnp.exp2(s - m_new))
            m = m_new
        if j == n_tiles - 1:
            heads.append(acc[:, :MLA_V_DIM] / acc[:, MLA_V_DIM:MLA_V_DIM + 1])
    y = jnp.concatenate(heads, axis=1)
    out_ref[...] = _rms_norm(y, gain_ref[...]).astype(BF16)


def _attention(qc, kc, vc, lw, layer, bsz, seq):
    tq, tk = min(ATTN_TQ, seq), min(ATTN_TK, seq)
    nq = seq // tq
    hb4 = MLA_HEADS * HEAD_BLOCK
    kv_spec = pl.BlockSpec((seq, hb4), lambda b, i: (b, 0))
    return pl.pallas_call(
        functools.partial(_attn_kernel, seq=seq, tk=tk), grid=(bsz, nq),
        in_specs=[pl.BlockSpec((tq, hb4), lambda b, i: (b * nq + i, 0)), kv_spec, kv_spec,
                  _layer_spec((1, W_GRP), layer, block=(0, GROUP_C))],
        out_specs=pl.BlockSpec((tq, W_GRP), lambda b, i: (b * nq + i, 0)),
        out_shape=jax.ShapeDtypeStruct((bsz * seq, W_GRP), BF16),
        scratch_shapes=[pltpu.VMEM((tq, tk), F32), pltpu.VMEM((tq, tk), F32)],
        compiler_params=_params("parallel", "parallel"), name="mla_attention",
    )(qc, kc, vc, lw["g_branch"])


def _outproj_kernel(ab_ref, ap_ref, app_ref, apn_ref, du_ref, dup_ref, dun_ref, yb_ref, yc_ref, h_ref,
                    wsc_ref, wcfm_ref, gcfm_ref, bcfm_ref, gain_ref, wout_ref, g1_ref, b1_ref,
                    out_ref, pext_ref, uext_ref, shift_ref, *, tiles_per_seq):
    tm = ab_ref.shape[0]
    i = pl.program_id(0) % tiles_per_seq
    not_first = (i != 0).astype(F32)
    not_last = (i != tiles_per_seq - 1).astype(F32)
    pext_ref[0:HALO, :] = app_ref[...] * not_first
    pext_ref[HALO:HALO + tm, :] = ap_ref[...]
    pext_ref[HALO + tm:, :] = apn_ref[...] * not_last
    uext_ref[0:HALO, :] = dup_ref[...] * not_first
    uext_ref[HALO:HALO + tm, :] = du_ref[...]
    uext_ref[HALO + tm:, :] = dun_ref[...] * not_last

    conv = jnp.zeros((tm, W_GRP), F32)
    for j in range(SC_WIDTH):
        conv = conv + pext_ref[pl.ds(HALO - SC_WIDTH // 2 + j, tm), :] * wsc_ref[j:j + 1, :]
    ya = _rms_norm(ab_ref[...] * conv, gain_ref[:, GROUP_A * W_GRP:(GROUP_A + 1) * W_GRP])
    first = HALO - CFM_WIDTH // 2
    span = SUBLANES * ((first + CFM_WIDTH - 1) // SUBLANES)
    assert shift_ref.shape[0] == tm + span and span + SUBLANES <= 2 * HALO
    u = jnp.zeros((tm, W_GRP), F32)
    for r in range(SUBLANES):
        shift_ref[...] = uext_ref[pl.ds(r, tm + span), :]
        for off in range(0, span + 1, SUBLANES):
            j = off + r - first
            if 0 <= j < CFM_WIDTH:
                u = u + shift_ref[pl.ds(off, tm), :] * wcfm_ref[j:j + 1, :]
    un = _layer_norm(u, gcfm_ref[...], bcfm_ref[...])
    yd = _rms_norm(un * _sigmoid(un), gain_ref[:, GROUP_D * W_GRP:(GROUP_D + 1) * W_GRP])

    mix = (_dot(ya.astype(BF16), wout_ref[0:W_GRP, :])
           + _dot(yb_ref[...], wout_ref[W_GRP:2 * W_GRP, :])
           + _dot(yc_ref[...], wout_ref[2 * W_GRP:3 * W_GRP, :])
           + _dot(yd.astype(BF16), wout_ref[3 * W_GRP:4 * W_GRP, :]))
    out_ref[...] = _layer_norm(DN_ALPHA * h_ref[...] + mix, g1_ref[...], b1_ref[...])


def _halo_specs(tm, cols, halo, total_rows):
    per = tm // halo
    last = total_rows // halo - 1
    prev = pl.BlockSpec((halo, cols), lambda i: (jnp.maximum(i * per - 1, 0), 0))
    nxt = pl.BlockSpec((halo, cols), lambda i: (jnp.minimum((i + 1) * per, last), 0))
    return prev, nxt


def _outproj(ab, ap, du, yb, yc, h, lw, layer, seq, tm):
    t = h.shape[0]
    prev, nxt = _halo_specs(tm, W_GRP, HALO, t)
    g = _row_spec(tm, W_GRP)
    ls = functools.partial(_layer_spec, layer=layer)
    in_specs = [g, g, prev, nxt, g, prev, nxt, g, g, _row_spec(tm, D_MODEL),
                ls((SC_WIDTH, W_GRP)), ls((CFM_WIDTH, W_GRP)), ls((1, W_GRP)), ls((1, W_GRP)),
                ls((1, 4 * W_GRP)), ls((4 * W_GRP, D_MODEL)), ls((1, D_MODEL)), ls((1, D_MODEL))]
    return pl.pallas_call(
        functools.partial(_outproj_kernel, tiles_per_seq=seq // tm), grid=(t // tm,),
        in_specs=in_specs, out_specs=_row_spec(tm, D_MODEL),
        out_shape=jax.ShapeDtypeStruct((t, D_MODEL), F32),
        scratch_shapes=[pltpu.VMEM((tm + 2 * HALO, W_GRP), F32), pltpu.VMEM((tm + 2 * HALO, W_GRP), F32),
                        pltpu.VMEM((tm + 2 * HALO - SUBLANES, W_GRP), F32)],
        compiler_params=_params("parallel"), name="outproj",
    )(ab, ap, ap, ap, du, du, du, yb, yc, h, lw["w_sc_conv"], lw["w_cfm_dw"], lw["g_cfm_ln"],
      lw["b_cfm_ln"], lw["g_branch"], lw["w_out"], lw["ln1_g"], lw["ln1_b"])


def _ffn_kernel(h_ref, hp_ref, hn_ref, wup_ref, wdw_ref, wdown_ref, g2_ref, b2_ref, out_ref,
                hx_ref, acc_ref, ug0_ref, uv0_ref, ug1_ref, uv1_ref, *, tiles_per_seq, n_chunks):
    tm = h_ref.shape[0]
    i = pl.program_id(0) % tiles_per_seq
    not_first = (i != 0).astype(F32)
    not_last = (i != tiles_per_seq - 1).astype(F32)
    ext = tm + 2 * FFN_HALO
    hx_ref[0:FFN_HALO, :] = (hp_ref[...] * not_first).astype(BF16)
    hx_ref[FFN_HALO:FFN_HALO + tm, :] = h_ref[...].astype(BF16)
    hx_ref[FFN_HALO + tm:, :] = (hn_ref[...] * not_last).astype(BF16)
    acc_ref[...] = jnp.zeros_like(acc_ref)

    def conv3(u, w):
        prev = pltpu.roll(u, 1, 0)
        nxt = pltpu.roll(u, ext - 1, 0)
        full = prev * w[0:1, :] + u * w[1:2, :] + nxt * w[2:3, :]
        return full[FFN_HALO:FFN_HALO + tm, :]

    bufs = ((ug0_ref, uv0_ref), (ug1_ref, uv1_ref))

    def up_cols(c, half):
        return wup_ref[:, pl.ds(half * D_FF + c * FF_CHUNK, FF_CHUNK)]

    def up(slot, c):
        hx = hx_ref[...]
        bufs[slot][0][...] = _dot(hx, up_cols(c, 0))
        bufs[slot][1][...] = _dot(hx, up_cols(c, 1))

    def activation(slot, c):
        gate = conv3(bufs[slot][0][...], wdw_ref[0, c])
        val = conv3(bufs[slot][1][...], wdw_ref[1, c])
        return (gate * _sigmoid(gate) * val).astype(BF16)

    def down_rows(c, n):
        return wdown_ref[pl.ds(c * FF_CHUNK, n * FF_CHUNK), :]

    def body(jj, carry):
        c = 2 * jj
        up(1, c + 1)
        a0 = activation(0, c)
        up(0, c + 2)
        a1 = activation(1, c + 1)
        acc_ref[...] += _dot(jnp.concatenate([a0, a1], axis=1), down_rows(c, 2))
        return carry

    assert n_chunks % 2 == 1
    up(0, 0)
    for jj in range(n_chunks // 2):
        body(jj, 0)
    last = n_chunks - 1
    mix = acc_ref[...] + _dot(activation(0, last), down_rows(last, 1))
    out_ref[...] = _layer_norm(DN_ALPHA * h_ref[...] + mix, g2_ref[...], b2_ref[...])


def _ffn(h, lw, layer, seq, tm):
    t = h.shape[0]
    n_chunks = D_FF // FF_CHUNK
    prev, nxt = _halo_specs(tm, D_MODEL, FFN_HALO, t)
    ls = functools.partial(_layer_spec, layer=layer)
    in_specs = [_row_spec(tm, D_MODEL), prev, nxt,
                ls((D_MODEL, 2 * D_FF)), ls((2, n_chunks, FFN_CONV_WIDTH, FF_CHUNK)),
                ls((D_FF, D_MODEL)), ls((1, D_MODEL)), ls((1, D_MODEL))]
    ext = tm + 2 * FFN_HALO
    return pl.pallas_call(
        functools.partial(_ffn_kernel, tiles_per_seq=seq // tm, n_chunks=n_chunks), grid=(t // tm,),
        in_specs=in_specs, out_specs=_row_spec(tm, D_MODEL),
        out_shape=jax.ShapeDtypeStruct((t, D_MODEL), F32),
        scratch_shapes=[pltpu.VMEM((ext, D_MODEL), BF16), pltpu.VMEM((tm, D_MODEL), F32)]
        + [pltpu.VMEM((ext, FF_CHUNK), F32)] * 4,
        compiler_params=_params("parallel"), name="conv_ffn",
    )(h, h, h, lw["w_ffn_up"], lw["w_ffn_dw"], lw["w_ffn_down"], lw["ln2_g"], lw["ln2_b"])


def _head_blocks(w, per_head, lo, n):
    depth, rows = w.shape[:2]
    wh = w.reshape(depth, rows, -1, per_head)[..., lo:lo + n]
    return jnp.pad(wh, ((0, 0), (0, 0), (0, 0), (0, HEAD_BLOCK - n))).reshape(depth, rows, -1)


def _stack_weights(p):
    w_in = p["w_in"]
    depth = w_in.shape[0]
    sizes = (256, 256, 256, 128, 128, 256, 256, 16, 16, 256, 128, 32, 512)
    offs = [0]
    for s in sizes:
        offs.append(offs[-1] + s)
    seg = lambda idx: w_in[:, :, offs[idx]:offs[idx + 1]]
    zeros = lambda n: jnp.zeros((depth, D_MODEL, n), F32)
    misc = [seg(7), seg(8), zeros(ROPE_LO - 2 * GLA_LOWRANK), seg(11), zeros(LANES - ROPE_LO - MLA_ROPE)]
    w_in_p = jnp.concatenate([seg(0), seg(1), seg(2), seg(3), seg(4), seg(5), seg(6), *misc, seg(10), seg(9),
                              seg(12)], axis=2).astype(BF16)
    hk = GLA_HEADS * GLA_DK
    up = p["w_gla_a_up"]
    zk = jnp.zeros((depth, GLA_LOWRANK, hk), F32)
    w_gate = jnp.concatenate([jnp.concatenate([up[:, 0], zk], axis=2), jnp.concatenate([zk, up[:, 1]], axis=2),
                              jnp.zeros((depth, LANES - 2 * GLA_LOWRANK, 2 * hk), F32)], axis=1)
    w_ukv = p["w_mla_ukv"]
    n_ff = D_FF // FF_CHUNK
    w_dw = p["w_ffn_dw"].reshape(depth, FFN_CONV_WIDTH, 2, n_ff, FF_CHUNK).transpose(0, 2, 3, 1, 4)
    row = lambda a: a.reshape(depth, 1, -1)
    return {
        "w_in": w_in_p,
        "w_gate": w_gate.astype(BF16),
        "b_gate": row(p["b_gla_a"]),
        "g_mla_q": row(p["g_mla_q"]),
        "w_uq": _head_blocks(p["w_mla_uq"], MLA_NOPE + MLA_ROPE, 0, MLA_NOPE + MLA_ROPE).astype(BF16),
        "g_mla_kv": row(p["g_mla_kv"]),
        "w_uk": _head_blocks(w_ukv, MLA_NOPE + MLA_V_DIM, 0, MLA_NOPE).astype(BF16),
        "w_uv": _head_blocks(w_ukv, MLA_NOPE + MLA_V_DIM, MLA_NOPE, MLA_V_DIM).astype(BF16),
        "g_gla_head": row(p["g_gla_head"]),
        "w_sc_conv": p["w_sc_conv"],
        "w_cfm_dw": p["w_cfm_dw"],
        "g_cfm_ln": row(p["g_cfm_ln"]),
        "b_cfm_ln": row(p["b_cfm_ln"]),
        "g_branch": row(p["g_branch"]),
        "w_out": p["w_out"].astype(BF16),
        "ln1_g": row(p["ln1_g"]),
        "ln1_b": row(p["ln1_b"]),
        "w_ffn_up": p["w_ffn_up"].astype(BF16),
        "w_ffn_dw": w_dw,
        "w_ffn_down": p["w_ffn_down"].astype(BF16),
        "ln2_g": row(p["ln2_g"]),
        "ln2_b": row(p["ln2_b"]),
    }


def kernel(x, positions, ln_in_g, ln_in_b, w_in, w_sc_conv, w_gla_a_up, b_gla_a, g_gla_head, g_mla_q, w_mla_uq,
           g_mla_kv, w_mla_ukv, w_cfm_dw, g_cfm_ln, b_cfm_ln, g_branch, w_out, ln1_g, ln1_b, w_ffn_up, w_ffn_dw,
           w_ffn_down, ln2_g, ln2_b):
    p = dict(w_in=w_in, w_sc_conv=w_sc_conv, w_gla_a_up=w_gla_a_up, b_gla_a=b_gla_a, g_gla_head=g_gla_head,
             g_mla_q=g_mla_q, w_mla_uq=w_mla_uq, g_mla_kv=g_mla_kv, w_mla_ukv=w_mla_ukv, w_cfm_dw=w_cfm_dw,
             g_cfm_ln=g_cfm_ln, b_cfm_ln=b_cfm_ln, g_branch=g_branch, w_out=w_out, ln1_g=ln1_g, ln1_b=ln1_b,
             w_ffn_up=w_ffn_up, w_ffn_dw=w_ffn_dw, w_ffn_down=w_ffn_down, ln2_g=ln2_g, ln2_b=ln2_b)
    bsz, seq, d = x.shape
    assert d == D_MODEL and seq % GLA_CHUNK == 0
    tm = min(TOKEN_TILE, seq)
    assert seq % tm == 0 and tm % HALO == 0
    t = bsz * seq
    rope = _rope_tables(positions, tm)
    h = x.reshape(t, d)
    lw = _stack_weights(p)
    for l in range(DEPTH):
        if l == 0:
            h, *mixer_in = _inproj(h, lw, l, rope, tm, input_norm=(ln_in_g, ln_in_b))
        else:
            mixer_in = _inproj(h, lw, l, rope, tm)
        ab, ap, du, q, k, v, gg, laf, lab, qc, kc, vc = mixer_in
        yb = _gla(q, k, v, laf, lab, gg, lw, l, bsz, seq)
        yc = _attention(qc, kc, vc, lw, l, bsz, seq)
        h = _outproj(ab, ap, du, yb, yc, h, lw, l, seq, tm)
        h = _ffn(h, lw, l, seq, min(FFN_TILE, seq))
    return h.reshape(bsz, seq, d)
```

```python
import functools
import math

import jax
import jax.numpy as jnp
from jax import lax
from jax.experimental import pallas as pl
from jax.experimental.pallas import tpu as pltpu

F32 = jnp.float32
BF16 = jnp.bfloat16

D_MODEL = 1024
DEPTH = 2
W_GRP = 256
GROUP_A, GROUP_B, GROUP_C, GROUP_D = range(4)
SC_WIDTH = 3
GLA_HEADS = 4
GLA_DK = 32
GLA_DV = 64
GLA_LOWRANK = 16
GLA_TAU = 16.0
MLA_HEADS = 4
MLA_V_DIM = 64
MLA_NOPE = 64
MLA_ROPE = 32
MLA_Q_RANK = 256
MLA_KV_RANK = 128
ROPE_BASE = 10000.0
CFM_WIDTH = 31
D_FF = 2816
FFN_CONV_WIDTH = 3
DN_ALPHA = (2.0 * DEPTH) ** 0.25
EPS = 1e-5

LANES = 128
SUBLANES = 8
VMEM_LIMIT_BYTES = 56 * 1024 * 1024

TOKEN_TILE = 512
FFN_TILE = 1024
HALO = 16
FFN_HALO = SUBLANES
FF_CHUNK = 256
GLA_CHUNK = 64
GLA_REF_ROW = GLA_CHUNK // 2
GLA_GROUP = 4
ATTN_TQ = 512
ATTN_TK = 2048
HEAD_BLOCK = LANES
ROPE_LO = MLA_NOPE
ROPE_HALF = MLA_ROPE // 2
LOG2E = math.log2(math.e)

COL_SC_B, COL_SC_C, COL_SC_H = 0, 256, 512
COL_GQK, COL_GV, COL_GG = 768, 1024, 1280
COL_MISC_CKV = 1536
COL_CQ = 1792
COL_CFM_A, COL_CFM_G = 2048, 2304
D_IN_PAD = 2560


def _params(*sem):
    return pltpu.CompilerParams(dimension_semantics=sem, vmem_limit_bytes=VMEM_LIMIT_BYTES)


def _dot(a, b):
    return jnp.dot(a, b, preferred_element_type=F32)


def _dot_nt(a, b):
    return lax.dot_general(a, b, (((1,), (1,)), ((), ())), preferred_element_type=F32)


def _dot_tn(a, b):
    return lax.dot_general(a, b, (((0,), (0,)), ((), ())), preferred_element_type=F32)


def _layer_norm(x, g, b):
    mu = jnp.mean(x, axis=-1, keepdims=True)
    xc = x - mu
    var = jnp.mean(xc * xc, axis=-1, keepdims=True)
    return xc * lax.rsqrt(var + EPS) * g + b


def _rms_norm(x, g):
    return x * lax.rsqrt(jnp.mean(x * x, axis=-1, keepdims=True) + EPS) * g


def _sigmoid(x):
    return 1.0 / (1.0 + jnp.exp(-x))


def _log_sigmoid(x):
    return jnp.minimum(x, 0.0) - jnp.log1p(jnp.exp(-jnp.abs(x)))


def _const_spec(shape):
    return pl.BlockSpec(shape, lambda *_: (0,) * len(shape), pipeline_mode=pl.Buffered(1))


def _layer_spec(shape, layer, block=None):
    idx = (layer,) + (tuple(block) if block is not None else (0,) * len(shape))
    return pl.BlockSpec((None,) + tuple(shape), lambda *_: idx, pipeline_mode=pl.Buffered(1))


def _row_spec(rows, cols):
    return pl.BlockSpec((rows, cols), lambda i: (i, 0))


def _rope_kernel(pos_ref, inv_ref, c_ref, s1_ref, s2_ref):
    ang = inv_ref[...] * pos_ref[...].astype(F32)
    frow = lax.broadcasted_iota(jnp.int32, (ROPE_HALF, LANES), 0)
    lane = lax.broadcasted_iota(jnp.int32, (ROPE_HALF, LANES), 1)
    to_first = jnp.where(lane == ROPE_LO + frow, 1.0, 0.0).astype(BF16)
    to_second = jnp.where(lane == ROPE_LO + ROPE_HALF + frow, 1.0, 0.0).astype(BF16)

    def place(x, where_to):
        hi = x.astype(BF16)
        rem = x - hi.astype(F32)
        mid = rem.astype(BF16)
        lo = (rem - mid.astype(F32)).astype(BF16)
        return _dot_tn(hi, where_to) + _dot_tn(mid, where_to) + _dot_tn(lo, where_to)

    cos, sin = jnp.cos(ang), jnp.sin(ang)
    out_lane = lax.broadcasted_iota(jnp.int32, c_ref.shape, 1)
    c_ref[...] = jnp.where(out_lane < ROPE_LO, 1.0, place(cos, to_first + to_second))
    s1_ref[...] = place(sin, to_second)
    s2_ref[...] = -place(sin, to_first)


def _rope_tables(positions, tm):
    t = positions.size
    inv = ROPE_BASE ** (-jnp.arange(0, MLA_ROPE, 2, dtype=F32) / MLA_ROPE)
    out = jax.ShapeDtypeStruct((t, LANES), F32)
    return pl.pallas_call(
        _rope_kernel, grid=(t // tm,),
        in_specs=[pl.BlockSpec((1, tm), lambda i: (0, i)), _const_spec((ROPE_HALF, 1))],
        out_specs=[_row_spec(tm, LANES)] * 3, out_shape=[out] * 3,
        compiler_params=_params("parallel"), name="rope_tables",
    )(positions.reshape(1, t), inv.reshape(ROPE_HALF, 1))


def _value_lanes(head):
    assert 2 * MLA_V_DIM == HEAD_BLOCK
    return (0, MLA_V_DIM) if head % 2 == 0 else (MLA_V_DIM, 0)


def _rope(x, c, s1, s2):
    return (x * c + pltpu.roll(x, ROPE_HALF, 1) * s1
            + pltpu.roll(x, LANES - ROPE_HALF, 1) * s2)


def _inproj_kernel(*refs, pre_norm):
    refs = list(refs)
    x_ref = refs.pop(0)
    if pre_norm:
        lng_ref, lnb_ref = refs.pop(0), refs.pop(0)
    (w_ref, wg_ref, bg_ref, gq_ref, wuq_ref, gkv_ref, wuk_ref, wuv_ref, c_ref, s1_ref, s2_ref) = refs[:11]
    outs = refs[11:]
    if pre_norm:
        hn_ref = outs.pop(0)
        h = _layer_norm(x_ref[...], lng_ref[...], lnb_ref[...])
        hn_ref[...] = h
    else:
        h = x_ref[...]
    (ab_ref, ap_ref, du_ref, q_ref, k_ref, v_ref, gg_ref, laf_ref, lab_ref, qc_ref, kc_ref, vc_ref) = outs
    hb = h.astype(BF16)

    def proj(lo, n):
        return _dot(hb, w_ref[:, lo:lo + n])

    misc_ckv = proj(COL_MISC_CKV, 2 * LANES)
    cq = proj(COL_CQ, MLA_Q_RANK)
    misc = misc_ckv[:, :LANES]
    pre = _dot(misc.astype(BF16), wg_ref[...]) + bg_ref[...]
    la = _log_sigmoid(pre) * (1.0 / GLA_TAU)
    laf_ref[...] = la[:, :LANES]
    lab_ref[...] = la[:, LANES:]
    c, s1, s2 = c_ref[...], s1_ref[...], s2_ref[...]
    cqn = _rms_norm(cq, gq_ref[...]).astype(BF16)
    qm = _dot(cqn, wuq_ref[...])
    qscale = (MLA_NOPE + MLA_ROPE) ** -0.5 * LOG2E
    ckvn = _rms_norm(misc_ckv[:, LANES:], gkv_ref[...]).astype(BF16)
    kn = _dot(ckvn, wuk_ref[...])
    vn = _dot(ckvn, wuv_ref[...])
    lane = lax.broadcasted_iota(jnp.int32, misc.shape, 1)
    kr = _rope(jnp.where(lane >= ROPE_LO, misc, 0.0), c, s1, s2)
    for hh in range(MLA_HEADS):
        sl = slice(hh * HEAD_BLOCK, (hh + 1) * HEAD_BLOCK)
        qc_ref[:, sl] = (_rope(qm[:, sl], c, s1, s2) * qscale).astype(BF16)
        kc_ref[:, sl] = (kn[:, sl] + kr).astype(BF16)
        ones_col = jnp.where(lane == _value_lanes(hh)[1], 1.0, 0.0)
        vc_ref[:, sl] = (vn[:, sl] + ones_col).astype(BF16)
    ab_ref[...] = proj(COL_SC_B, W_GRP)
    ap_ref[...] = proj(COL_SC_C, W_GRP) * proj(COL_SC_H, W_GRP)
    du_ref[...] = proj(COL_CFM_A, W_GRP) * _sigmoid(proj(COL_CFM_G, W_GRP))
    qk = proj(COL_GQK, 2 * LANES)
    q_ref[...] = qk[:, :LANES] * (GLA_DK ** -0.5)
    k_ref[...] = qk[:, LANES:]
    v_ref[...] = proj(COL_GV, W_GRP)
    gg_ref[...] = proj(COL_GG, W_GRP)


def _inproj(h, lw, layer, rope, tm, input_norm=None):
    t = h.shape[0]
    f = lambda n, dt=F32: jax.ShapeDtypeStruct((t, n), dt)
    hb4 = MLA_HEADS * HEAD_BLOCK
    out_shape = [f(W_GRP), f(W_GRP), f(W_GRP), f(LANES), f(LANES), f(W_GRP), f(W_GRP), f(LANES), f(LANES),
                 f(hb4, BF16), f(hb4, BF16), f(hb4, BF16)]
    ls = functools.partial(_layer_spec, layer=layer)
    in_specs = [ls((D_MODEL, D_IN_PAD)), ls((LANES, 2 * LANES)), ls((1, 2 * LANES)),
                ls((1, MLA_Q_RANK)), ls((MLA_Q_RANK, hb4)),
                ls((1, MLA_KV_RANK)), ls((MLA_KV_RANK, hb4)), ls((MLA_KV_RANK, hb4)),
                _row_spec(tm, LANES), _row_spec(tm, LANES), _row_spec(tm, LANES)]
    args = [lw["w_in"], lw["w_gate"], lw["b_gate"], lw["g_mla_q"], lw["w_uq"], lw["g_mla_kv"], lw["w_uk"],
            lw["w_uv"], *rope]
    pre_norm = input_norm is not None
    if pre_norm:
        out_shape.insert(0, f(D_MODEL))
        in_specs = [_const_spec((1, D_MODEL)), _const_spec((1, D_MODEL))] + in_specs
        args = [a.reshape(1, D_MODEL) for a in input_norm] + args
    out_specs = [_row_spec(tm, s.shape[1]) for s in out_shape]
    return pl.pallas_call(
        functools.partial(_inproj_kernel, pre_norm=pre_norm), grid=(t // tm,),
        in_specs=[_row_spec(tm, D_MODEL)] + in_specs, out_specs=out_specs, out_shape=out_shape,
        compiler_params=_params("parallel"), name="inproj",
    )(h, *args)


def _gla_kernel(q_ref, k_ref, v_ref, laf_ref, lab_ref, gg_ref, ghead_ref, gain_ref, out_ref,
                acc_ref, sf_ref, sb_ref, *, seq):
    cs = GLA_CHUNK
    grp = min(GLA_GROUP, seq // cs)
    gr = grp * cs
    n_groups = seq // gr
    hk = GLA_HEADS * GLA_DK
    hv = GLA_HEADS * GLA_DV
    row = lax.broadcasted_iota(jnp.int32, (gr, gr), 0)
    col = lax.broadcasted_iota(jnp.int32, (gr, gr), 1)
    tri = jnp.where((row >= col) & (row // cs == col // cs), 1.0, 0.0).astype(BF16)
    srow = lax.broadcasted_iota(jnp.int32, (GLA_HEADS * cs, cs), 0) % cs
    scol = lax.broadcasted_iota(jnp.int32, (GLA_HEADS * cs, cs), 1)
    keep = (srow >= scol, srow < scol)
    qlane_head = lax.broadcasted_iota(jnp.int32, (cs, hk), 1) // GLA_DK
    olane_head = lax.broadcasted_iota(jnp.int32, (cs, hv), 1) // GLA_DV
    srow_head = lax.broadcasted_iota(jnp.int32, (hk, hv), 0) // GLA_DK
    scol_head = lax.broadcasted_iota(jnp.int32, (hk, hv), 1) // GLA_DV
    eye = (lax.broadcasted_iota(jnp.int32, (hk, hk), 0) == lax.broadcasted_iota(jnp.int32, (hk, hk), 1))

    acc_ref[...] = jnp.zeros_like(acc_ref)
    sf_ref[...] = jnp.zeros_like(sf_ref)
    sb_ref[...] = jnp.zeros_like(sb_ref)

    def decay_phase(g, direction):
        la_ref = (laf_ref, lab_ref)[direction]
        base = pl.multiple_of(g * gr, gr)
        la = la_ref[pl.ds(base, gr), :]
        hi = la.astype(BF16)
        rem = la - hi.astype(F32)
        mid = rem.astype(BF16)
        lo = (rem - mid.astype(F32)).astype(BF16)
        cum = _dot(tri, hi) + _dot(tri, mid) + _dot(tri, lo)
        return base, la, cum

    def chunk_phase(direction, base, la, cum):
        rows = pl.ds(base, gr)
        q, k = q_ref[rows, :], k_ref[rows, :]
        vb = v_ref[rows, :].astype(BF16)
        p, qin, vbs, kv, decay = [], [], [], [], []
        for i in range(grp):
            sl = slice(i * cs, (i + 1) * cs)
            total = cum[(i + 1) * cs - 1:(i + 1) * cs, :]
            b = cum[sl] if direction == 0 else total - cum[sl] + la[sl]
            ref = b[GLA_REF_ROW:GLA_REF_ROW + 1, :]
            qt = q[sl] * jnp.exp(b - ref)
            kt = k[sl] * jnp.exp(ref - b)
            qbd = jnp.concatenate([jnp.where(qlane_head == hh, qt, 0.0) for hh in range(GLA_HEADS)], axis=0)
            scores = _dot_nt(qbd.astype(BF16), kt.astype(BF16))
            p.append(jnp.where(keep[direction], scores, 0.0).astype(BF16))
            qin.append((qbd * jnp.exp(ref)).astype(BF16))
            khat = (kt * jnp.exp(total - ref)).astype(BF16)
            vbs.append(vb[sl])
            kv.append(jnp.where(srow_head == scol_head, _dot_tn(khat, vb[sl]), 0.0))
            decay.append(jnp.sum(jnp.where(eye, jnp.broadcast_to(jnp.exp(total), (hk, hk)), 0.0),
                                 axis=1, keepdims=True))
        return p, qin, vbs, kv, decay

    def state_phase(direction, base, p, qin, vbs, kv, decay):
        s_ref = (sf_ref, sb_ref)[direction]
        state = s_ref[...]
        for i in (range(grp) if direction == 0 else reversed(range(grp))):
            lhs = jnp.concatenate([qin[i], p[i]], axis=1)
            rhs = jnp.concatenate([state.astype(BF16), vbs[i]], axis=0)
            o_all = _dot(lhs, rhs)
            o = jnp.zeros((cs, hv), F32)
            for hh in range(GLA_HEADS):
                o = o + jnp.where(olane_head == hh, o_all[hh * cs:(hh + 1) * cs, :], 0.0)
            acc_ref[pl.ds(base + i * cs, cs), :] += o
            state = state * decay[i] + kv[i]
        s_ref[...] = state

    hrow = lax.broadcasted_iota(jnp.int32, (hv, hv), 0) // GLA_DV
    hcol = lax.broadcasted_iota(jnp.int32, (hv, hv), 1) // GLA_DV
    head_ones = jnp.where(hrow == hcol, 1.0, 0.0).astype(BF16)

    def finish(base):
        rows = pl.ds(base, gr)
        o = acc_ref[rows, :]
        sq = o * o
        hi = sq.astype(BF16)
        lo = (sq - hi.astype(F32)).astype(BF16)
        ms = (_dot(hi, head_ones) + _dot(lo, head_ones)) * (1.0 / GLA_DV)
        on = o * lax.rsqrt(ms + EPS) * ghead_ref[...]
        gg = gg_ref[rows, :]
        yb = on * (gg * _sigmoid(gg))
        out_ref[rows, :] = _rms_norm(yb, gain_ref[...]).astype(BF16)

    def finish_step(n):
        finish(pl.multiple_of(n * gr, gr))
        finish(pl.multiple_of((n_groups - 1 - n) * gr, gr))

    def step(n, finish_previous):
        if finish_previous:
            finish_step(n - 1)
        fwd = decay_phase(n, 0)
        bwd = decay_phase(n_groups - 1 - n, 1)
        fwd_chunks = chunk_phase(0, *fwd)
        bwd_chunks = chunk_phase(1, *bwd)
        state_phase(0, fwd[0], *fwd_chunks)
        state_phase(1, bwd[0], *bwd_chunks)

    assert n_groups % 2 == 0
    half = n_groups // 2
    lax.fori_loop(0, half + 1, lambda n, c: (step(n, False), c)[1], 0)
    lax.fori_loop(half + 1, n_groups, lambda n, c: (step(n, True), c)[1], 0)
    finish_step(n_groups - 1)


def _gla(q, k, v, laf, lab, gg, lw, layer, bsz, seq):
    hk, hv = GLA_HEADS * GLA_DK, GLA_HEADS * GLA_DV
    seq_spec = lambda n: pl.BlockSpec((seq, n), lambda b: (b, 0))
    return pl.pallas_call(
        functools.partial(_gla_kernel, seq=seq), grid=(bsz,),
        in_specs=[seq_spec(hk), seq_spec(hk), seq_spec(hv), seq_spec(hk), seq_spec(hk), seq_spec(hv),
                  _layer_spec((1, hv), layer), _layer_spec((1, W_GRP), layer, block=(0, GROUP_B))],
        out_specs=seq_spec(hv), out_shape=jax.ShapeDtypeStruct((bsz * seq, hv), BF16),
        scratch_shapes=[pltpu.VMEM((seq, hv), F32), pltpu.VMEM((hk, hv), F32), pltpu.VMEM((hk, hv), F32)],
        compiler_params=_params("parallel"), name="gla",
    )(q, k, v, laf, lab, gg, lw["g_gla_head"], lw["g_branch"])


def _attn_kernel(q_ref, k_ref, v_ref, gain_ref, out_ref, s0_ref, s1_ref, *, seq, tk):
    n_tiles = seq // tk
    bufs = (s0_ref, s1_ref)
    tiles = [(hh, j) for hh in range(MLA_HEADS) for j in range(n_tiles)]

    def scores(dst_ref, hh, j):
        sl = slice(hh * HEAD_BLOCK, (hh + 1) * HEAD_BLOCK)
        dst_ref[...] = _dot_nt(q_ref[:, sl], k_ref[j * tk:(j + 1) * tk, sl])

    heads = []
    m = acc = None
    scores(bufs[0], *tiles[0])
    for idx, (hh, j) in enumerate(tiles):
        if idx + 1 < len(tiles):
            scores(bufs[(idx + 1) % 2], *tiles[idx + 1])
        s = bufs[idx % 2][...]
        pv = lambda p: _dot(p.astype(BF16), v_ref[j * tk:(j + 1) * tk, hh * HEAD_BLOCK:(hh + 1) * HEAD_BLOCK])
        m_tile = jnp.max(s, axis=1, keepdims=True)
        if j == 0:
            m = m_tile
            acc = pv(jnp.exp2(s - m))
        else:
            m_new = jnp.maximum(m, m_tile)
            acc = acc * jnp.exp2(m - m_new) + pv(jnp.exp2(s - m_new))
            m = m_new
        if j == n_tiles - 1:
            ones_lane = _value_lanes(hh)[1]
            heads.append(acc / acc[:, ones_lane:ones_lane + 1])
    lane = lax.broadcasted_iota(jnp.int32, heads[0].shape, 1)
    low_half = lane < MLA_V_DIM
    y = jnp.concatenate([jnp.where(low_half, heads[hh], heads[hh + 1]) for hh in range(0, MLA_HEADS, 2)], axis=1)
    out_ref[...] = _rms_norm(y, gain_ref[...]).astype(BF16)


def _attention(qc, kc, vc, lw, layer, bsz, seq):
    tq, tk = min(ATTN_TQ, seq), min(ATTN_TK, seq)
    nq = seq // tq
    hb4 = MLA_HEADS * HEAD_BLOCK
    kv_spec = pl.BlockSpec((seq, hb4), lambda b, i: (b, 0))
    return pl.pallas_call(
        functools.partial(_attn_kernel, seq=seq, tk=tk), grid=(bsz, nq),
        in_specs=[pl.BlockSpec((tq, hb4), lambda b, i: (b * nq + i, 0)), kv_spec, kv_spec,
                  _layer_spec((1, W_GRP), layer, block=(0, GROUP_C))],
        out_specs=pl.BlockSpec((tq, W_GRP), lambda b, i: (b * nq + i, 0)),
        out_shape=jax.ShapeDtypeStruct((bsz * seq, W_GRP), BF16),
        scratch_shapes=[pltpu.VMEM((tq, tk), F32), pltpu.VMEM((tq, tk), F32)],
        compiler_params=_params("parallel", "parallel"), name="mla_attention",
    )(qc, kc, vc, lw["g_branch"])


def _outproj_kernel(ab_ref, ap_ref, app_ref, apn_ref, du_ref, dup_ref, dun_ref, yb_ref, yc_ref, h_ref,
                    wsc_ref, wcfm_ref, gcfm_ref, bcfm_ref, gain_ref, wout_ref, g1_ref, b1_ref,
                    out_ref, pext_ref, uext_ref, shift_ref, *, tiles_per_seq):
    tm = ab_ref.shape[0]
    i = pl.program_id(0) % tiles_per_seq
    not_first = (i != 0).astype(F32)
    not_last = (i != tiles_per_seq - 1).astype(F32)
    pext_ref[0:HALO, :] = app_ref[...] * not_first
    pext_ref[HALO:HALO + tm, :] = ap_ref[...]
    pext_ref[HALO + tm:, :] = apn_ref[...] * not_last
    uext_ref[0:HALO, :] = dup_ref[...] * not_first
    uext_ref[HALO:HALO + tm, :] = du_ref[...]
    uext_ref[HALO + tm:, :] = dun_ref[...] * not_last

    conv = jnp.zeros((tm, W_GRP), F32)
    for j in range(SC_WIDTH):
        conv = conv + pext_ref[pl.ds(HALO - SC_WIDTH // 2 + j, tm), :] * wsc_ref[j:j + 1, :]
    ya = _rms_norm(ab_ref[...] * conv, gain_ref[:, GROUP_A * W_GRP:(GROUP_A + 1) * W_GRP])
    first = HALO - CFM_WIDTH // 2
    span = SUBLANES * ((first + CFM_WIDTH - 1) // SUBLANES)
    assert shift_ref.shape[0] == tm + span and span + SUBLANES <= 2 * HALO
    u = jnp.zeros((tm, W_GRP), F32)
    for r in range(SUBLANES):
        shift_ref[...] = uext_ref[pl.ds(r, tm + span), :]
        for off in range(0, span + 1, SUBLANES):
            j = off + r - first
            if 0 <= j < CFM_WIDTH:
                u = u + shift_ref[pl.ds(off, tm), :] * wcfm_ref[j:j + 1, :]
    un = _layer_norm(u, gcfm_ref[...], bcfm_ref[...])
    yd = _rms_norm(un * _sigmoid(un), gain_ref[:, GROUP_D * W_GRP:(GROUP_D + 1) * W_GRP])

    mix = (_dot(ya.astype(BF16), wout_ref[0:W_GRP, :])
           + _dot(yb_ref[...], wout_ref[W_GRP:2 * W_GRP, :])
           + _dot(yc_ref[...], wout_ref[2 * W_GRP:3 * W_GRP, :])
           + _dot(yd.astype(BF16), wout_ref[3 * W_GRP:4 * W_GRP, :]))
    out_ref[...] = _layer_norm(DN_ALPHA * h_ref[...] + mix, g1_ref[...], b1_ref[...])


def _halo_specs(tm, cols, halo, total_rows):
    per = tm // halo
    last = total_rows // halo - 1
    prev = pl.BlockSpec((halo, cols), lambda i: (jnp.maximum(i * per - 1, 0), 0))
    nxt = pl.BlockSpec((halo, cols), lambda i: (jnp.minimum((i + 1) * per, last), 0))
    return prev, nxt


def _outproj(ab, ap, du, yb, yc, h, lw, layer, seq, tm):
    t = h.shape[0]
    prev, nxt = _halo_specs(tm, W_GRP, HALO, t)
    g = _row_spec(tm, W_GRP)
    ls = functools.partial(_layer_spec, layer=layer)
    in_specs = [g, g, prev, nxt, g, prev, nxt, g, g, _row_spec(tm, D_MODEL),
                ls((SC_WIDTH, W_GRP)), ls((CFM_WIDTH, W_GRP)), ls((1, W_GRP)), ls((1, W_GRP)),
                ls((1, 4 * W_GRP)), ls((4 * W_GRP, D_MODEL)), ls((1, D_MODEL)), ls((1, D_MODEL))]
    return pl.pallas_call(
        functools.partial(_outproj_kernel, tiles_per_seq=seq // tm), grid=(t // tm,),
        in_specs=in_specs, out_specs=_row_spec(tm, D_MODEL),
        out_shape=jax.ShapeDtypeStruct((t, D_MODEL), F32),
        scratch_shapes=[pltpu.VMEM((tm + 2 * HALO, W_GRP), F32), pltpu.VMEM((tm + 2 * HALO, W_GRP), F32),
                        pltpu.VMEM((tm + 2 * HALO - SUBLANES, W_GRP), F32)],
        compiler_params=_params("parallel"), name="outproj",
    )(ab, ap, ap, ap, du, du, du, yb, yc, h, lw["w_sc_conv"], lw["w_cfm_dw"], lw["g_cfm_ln"],
      lw["b_cfm_ln"], lw["g_branch"], lw["w_out"], lw["ln1_g"], lw["ln1_b"])


def _ffn_kernel(h_ref, hp_ref, hn_ref, wup_ref, wdw_ref, wdown_ref, g2_ref, b2_ref, out_ref,
                hx_ref, acc_ref, ug0_ref, uv0_ref, ug1_ref, uv1_ref, *, tiles_per_seq, n_chunks):
    tm = h_ref.shape[0]
    i = pl.program_id(0) % tiles_per_seq
    not_first = (i != 0).astype(F32)
    not_last = (i != tiles_per_seq - 1).astype(F32)
    ext = tm + 2 * FFN_HALO
    hx_ref[0:FFN_HALO, :] = (hp_ref[...] * not_first).astype(BF16)
    hx_ref[FFN_HALO:FFN_HALO + tm, :] = h_ref[...].astype(BF16)
    hx_ref[FFN_HALO + tm:, :] = (hn_ref[...] * not_last).astype(BF16)
    acc_ref[...] = jnp.zeros_like(acc_ref)

    def conv3(u, w):
        prev = pltpu.roll(u, 1, 0)
        nxt = pltpu.roll(u, ext - 1, 0)
        full = prev * w[0:1, :] + u * w[1:2, :] + nxt * w[2:3, :]
        return full[FFN_HALO:FFN_HALO + tm, :]

    bufs = ((ug0_ref, uv0_ref), (ug1_ref, uv1_ref))

    def up_cols(c, half):
        return wup_ref[:, pl.ds(half * D_FF + c * FF_CHUNK, FF_CHUNK)]

    def up(slot, c):
        hx = hx_ref[...]
        bufs[slot][0][...] = _dot(hx, up_cols(c, 0))
        bufs[slot][1][...] = _dot(hx, up_cols(c, 1))

    def activation(slot, c):
        gate = conv3(bufs[slot][0][...], wdw_ref[0, c])
        val = conv3(bufs[slot][1][...], wdw_ref[1, c])
        return (gate * _sigmoid(gate) * val).astype(BF16)

    def down_rows(c, n):
        return wdown_ref[pl.ds(c * FF_CHUNK, n * FF_CHUNK), :]

    def body(jj, carry):
        c = 2 * jj
        up(1, c + 1)
        a0 = activation(0, c)
        up(0, c + 2)
        a1 = activation(1, c + 1)
        acc_ref[...] += _dot(jnp.concatenate([a0, a1], axis=1), down_rows(c, 2))
        return carry

    assert n_chunks % 2 == 1
    up(0, 0)
    for jj in range(n_chunks // 2):
        body(jj, 0)
    last = n_chunks - 1
    mix = acc_ref[...] + _dot(activation(0, last), down_rows(last, 1))
    out_ref[...] = _layer_norm(DN_ALPHA * h_ref[...] + mix, g2_ref[...], b2_ref[...])


def _ffn(h, lw, layer, seq, tm):
    t = h.shape[0]
    n_chunks = D_FF // FF_CHUNK
    prev, nxt = _halo_specs(tm, D_MODEL, FFN_HALO, t)
    ls = functools.partial(_layer_spec, layer=layer)
    in_specs = [_row_spec(tm, D_MODEL), prev, nxt,
                ls((D_MODEL, 2 * D_FF)), ls((2, n_chunks, FFN_CONV_WIDTH, FF_CHUNK)),
                ls((D_FF, D_MODEL)), ls((1, D_MODEL)), ls((1, D_MODEL))]
    ext = tm + 2 * FFN_HALO
    return pl.pallas_call(
        functools.partial(_ffn_kernel, tiles_per_seq=seq // tm, n_chunks=n_chunks), grid=(t // tm,),
        in_specs=in_specs, out_specs=_row_spec(tm, D_MODEL),
        out_shape=jax.ShapeDtypeStruct((t, D_MODEL), F32),
        scratch_shapes=[pltpu.VMEM((ext, D_MODEL), BF16), pltpu.VMEM((tm, D_MODEL), F32)]
        + [pltpu.VMEM((ext, FF_CHUNK), F32)] * 4,
        compiler_params=_params("parallel"), name="conv_ffn",
    )(h, h, h, lw["w_ffn_up"], lw["w_ffn_dw"], lw["w_ffn_down"], lw["ln2_g"], lw["ln2_b"])


def _head_blocks(w, per_head, lo, n, value_layout=False):
    depth, rows = w.shape[:2]
    wh = w.reshape(depth, rows, -1, per_head)[..., lo:lo + n]
    blocks = jnp.pad(wh, ((0, 0), (0, 0), (0, 0), (0, HEAD_BLOCK - n)))
    if value_layout:
        shifted = jnp.pad(wh, ((0, 0), (0, 0), (0, 0), (HEAD_BLOCK - n, 0)))
        odd = (jnp.arange(wh.shape[2]) % 2 == 1)[None, None, :, None]
        blocks = jnp.where(odd, shifted, blocks)
    return blocks.reshape(depth, rows, -1)


def _stack_weights(p):
    w_in = p["w_in"]
    depth = w_in.shape[0]
    sizes = (256, 256, 256, 128, 128, 256, 256, 16, 16, 256, 128, 32, 512)
    offs = [0]
    for s in sizes:
        offs.append(offs[-1] + s)
    seg = lambda idx: w_in[:, :, offs[idx]:offs[idx + 1]]
    zeros = lambda n: jnp.zeros((depth, D_MODEL, n), F32)
    misc = [seg(7), seg(8), zeros(ROPE_LO - 2 * GLA_LOWRANK), seg(11), zeros(LANES - ROPE_LO - MLA_ROPE)]
    w_in_p = jnp.concatenate([seg(0), seg(1), seg(2), seg(3), seg(4), seg(5), seg(6), *misc, seg(10), seg(9),
                              seg(12)], axis=2).astype(BF16)
    hk = GLA_HEADS * GLA_DK
    up = p["w_gla_a_up"]
    zk = jnp.zeros((depth, GLA_LOWRANK, hk), F32)
    w_gate = jnp.concatenate([jnp.concatenate([up[:, 0], zk], axis=2), jnp.concatenate([zk, up[:, 1]], axis=2),
                              jnp.zeros((depth, LANES - 2 * GLA_LOWRANK, 2 * hk), F32)], axis=1)
    w_ukv = p["w_mla_ukv"]
    n_ff = D_FF // FF_CHUNK
    w_dw = p["w_ffn_dw"].reshape(depth, FFN_CONV_WIDTH, 2, n_ff, FF_CHUNK).transpose(0, 2, 3, 1, 4)
    row = lambda a: a.reshape(depth, 1, -1)
    return {
        "w_in": w_in_p,
        "w_gate": w_gate.astype(BF16),
        "b_gate": row(p["b_gla_a"]),
        "g_mla_q": row(p["g_mla_q"]),
        "w_uq": _head_blocks(p["w_mla_uq"], MLA_NOPE + MLA_ROPE, 0, MLA_NOPE + MLA_ROPE).astype(BF16),
        "g_mla_kv": row(p["g_mla_kv"]),
        "w_uk": _head_blocks(w_ukv, MLA_NOPE + MLA_V_DIM, 0, MLA_NOPE).astype(BF16),
        "w_uv": _head_blocks(w_ukv, MLA_NOPE + MLA_V_DIM, MLA_NOPE, MLA_V_DIM, value_layout=True).astype(BF16),
        "g_gla_head": row(p["g_gla_head"]),
        "w_sc_conv": p["w_sc_conv"],
        "w_cfm_dw": p["w_cfm_dw"],
        "g_cfm_ln": row(p["g_cfm_ln"]),
        "b_cfm_ln": row(p["b_cfm_ln"]),
        "g_branch": row(p["g_branch"]),
        "w_out": p["w_out"].astype(BF16),
        "ln1_g": row(p["ln1_g"]),
        "ln1_b": row(p["ln1_b"]),
        "w_ffn_up": p["w_ffn_up"].astype(BF16),
        "w_ffn_dw": w_dw,
        "w_ffn_down": p["w_ffn_down"].astype(BF16),
        "ln2_g": row(p["ln2_g"]),
        "ln2_b": row(p["ln2_b"]),
    }


def kernel(x, positions, ln_in_g, ln_in_b, w_in, w_sc_conv, w_gla_a_up, b_gla_a, g_gla_head, g_mla_q, w_mla_uq,
           g_mla_kv, w_mla_ukv, w_cfm_dw, g_cfm_ln, b_cfm_ln, g_branch, w_out, ln1_g, ln1_b, w_ffn_up, w_ffn_dw,
           w_ffn_down, ln2_g, ln2_b):
    p = dict(w_in=w_in, w_sc_conv=w_sc_conv, w_gla_a_up=w_gla_a_up, b_gla_a=b_gla_a, g_gla_head=g_gla_head,
             g_mla_q=g_mla_q, w_mla_uq=w_mla_uq, g_mla_kv=g_mla_kv, w_mla_ukv=w_mla_ukv, w_cfm_dw=w_cfm_dw,
             g_cfm_ln=g_cfm_ln, b_cfm_ln=b_cfm_ln, g_branch=g_branch, w_out=w_out, ln1_g=ln1_g, ln1_b=ln1_b,
             w_ffn_up=w_ffn_up, w_ffn_dw=w_ffn_dw, w_ffn_down=w_ffn_down, ln2_g=ln2_g, ln2_b=ln2_b)
    bsz, seq, d = x.shape
    assert d == D_MODEL and seq % GLA_CHUNK == 0
    tm = min(TOKEN_TILE, seq)
    assert seq % tm == 0 and tm % HALO == 0
    t = bsz * seq
    rope = _rope_tables(positions, tm)
    h = x.reshape(t, d)
    lw = _stack_weights(p)
    for l in range(DEPTH):
        if l == 0:
            h, *mixer_in = _inproj(h, lw, l, rope, tm, input_norm=(ln_in_g, ln_in_b))
        else:
            mixer_in = _inproj(h, lw, l, rope, tm)
        ab, ap, du, q, k, v, gg, laf, lab, qc, kc, vc = mixer_in
        yb = _gla(q, k, v, laf, lab, gg, lw, l, bsz, seq)
        yc = _attention(qc, kc, vc, lw, l, bsz, seq)
        h = _outproj(ab, ap, du, yb, yc, h, lw, l, seq, tm)
        h = _ffn(h, lw, l, seq, min(FFN_TILE, seq))
    return h.reshape(bsz, seq, d)
```

```python
import functools
import math

import jax
import jax.numpy as jnp
from jax import lax
from jax.experimental import pallas as pl
from jax.experimental.pallas import tpu as pltpu

F32 = jnp.float32
BF16 = jnp.bfloat16

D_MODEL = 1024
DEPTH = 2
W_GRP = 256
GROUP_A, GROUP_B, GROUP_C, GROUP_D = range(4)
SC_WIDTH = 3
GLA_HEADS = 4
GLA_DK = 32
GLA_DV = 64
GLA_LOWRANK = 16
GLA_TAU = 16.0
MLA_HEADS = 4
MLA_V_DIM = 64
MLA_NOPE = 64
MLA_ROPE = 32
MLA_Q_RANK = 256
MLA_KV_RANK = 128
ROPE_BASE = 10000.0
CFM_WIDTH = 31
D_FF = 2816
FFN_CONV_WIDTH = 3
DN_ALPHA = (2.0 * DEPTH) ** 0.25
EPS = 1e-5

LANES = 128
SUBLANES = 8
VMEM_LIMIT_BYTES = 56 * 1024 * 1024

TOKEN_TILE = 512
FFN_TILE = 1024
FFN_OUT_BLOCKS = 4
HALO = 16
OUTPROJ_ROW_BLOCKS = 2
FFN_HALO = SUBLANES
FF_CHUNK = 256
GLA_CHUNK = 64
GLA_REF_ROW = GLA_CHUNK // 2
GLA_GROUP = 4
ATTN_TQ = 512
ATTN_TK = 2048
HEAD_BLOCK = LANES
ROPE_LO = MLA_NOPE
ROPE_HALF = MLA_ROPE // 2
LOG2E = math.log2(math.e)

COL_SC_B, COL_SC_C, COL_SC_H = 0, 256, 512
COL_GQK, COL_GV, COL_GG = 768, 1024, 1280
COL_MISC_CKV = 1536
COL_CQ = 1792
COL_CFM_A, COL_CFM_G = 2048, 2304
D_IN_PAD = 2560


def _params(*sem):
    return pltpu.CompilerParams(dimension_semantics=sem, vmem_limit_bytes=VMEM_LIMIT_BYTES)


def _dot(a, b):
    return jnp.dot(a, b, preferred_element_type=F32)


def _dot_nt(a, b):
    return lax.dot_general(a, b, (((1,), (1,)), ((), ())), preferred_element_type=F32)


def _dot_tn(a, b):
    return lax.dot_general(a, b, (((0,), (0,)), ((), ())), preferred_element_type=F32)


def _layer_norm(x, g, b):
    mu = jnp.mean(x, axis=-1, keepdims=True)
    xc = x - mu
    var = jnp.mean(xc * xc, axis=-1, keepdims=True)
    return xc * lax.rsqrt(var + EPS) * g + b


def _rms_norm(x, g):
    return x * lax.rsqrt(jnp.mean(x * x, axis=-1, keepdims=True) + EPS) * g


def _sigmoid(x):
    return 1.0 / (1.0 + jnp.exp(-x))


def _log_sigmoid(x):
    return jnp.minimum(x, 0.0) - jnp.log1p(jnp.exp(-jnp.abs(x)))


def _const_spec(shape):
    return pl.BlockSpec(shape, lambda *_: (0,) * len(shape), pipeline_mode=pl.Buffered(1))


def _layer_spec(shape, layer, block=None):
    idx = (layer,) + (tuple(block) if block is not None else (0,) * len(shape))
    return pl.BlockSpec((None,) + tuple(shape), lambda *_: idx, pipeline_mode=pl.Buffered(1))


def _row_spec(rows, cols):
    return pl.BlockSpec((rows, cols), lambda i: (i, 0))


def _rope_kernel(pos_ref, inv_ref, c_ref, s1_ref, s2_ref):
    ang = inv_ref[...] * pos_ref[...].astype(F32)
    frow = lax.broadcasted_iota(jnp.int32, (ROPE_HALF, LANES), 0)
    lane = lax.broadcasted_iota(jnp.int32, (ROPE_HALF, LANES), 1)
    to_first = jnp.where(lane == ROPE_LO + frow, 1.0, 0.0).astype(BF16)
    to_second = jnp.where(lane == ROPE_LO + ROPE_HALF + frow, 1.0, 0.0).astype(BF16)

    def place(x, where_to):
        hi = x.astype(BF16)
        rem = x - hi.astype(F32)
        mid = rem.astype(BF16)
        lo = (rem - mid.astype(F32)).astype(BF16)
        return _dot_tn(hi, where_to) + _dot_tn(mid, where_to) + _dot_tn(lo, where_to)

    cos, sin = jnp.cos(ang), jnp.sin(ang)
    out_lane = lax.broadcasted_iota(jnp.int32, c_ref.shape, 1)
    c_ref[...] = jnp.where(out_lane < ROPE_LO, 1.0, place(cos, to_first + to_second))
    s1_ref[...] = place(sin, to_second)
    s2_ref[...] = -place(sin, to_first)


def _rope_tables(positions, tm):
    t = positions.size
    inv = ROPE_BASE ** (-jnp.arange(0, MLA_ROPE, 2, dtype=F32) / MLA_ROPE)
    out = jax.ShapeDtypeStruct((t, LANES), F32)
    return pl.pallas_call(
        _rope_kernel, grid=(t // tm,),
        in_specs=[pl.BlockSpec((1, tm), lambda i: (0, i)), _const_spec((ROPE_HALF, 1))],
        out_specs=[_row_spec(tm, LANES)] * 3, out_shape=[out] * 3,
        compiler_params=_params("parallel"), name="rope_tables",
    )(positions.reshape(1, t), inv.reshape(ROPE_HALF, 1))


def _value_lanes(head):
    assert 2 * MLA_V_DIM == HEAD_BLOCK
    return (0, MLA_V_DIM) if head % 2 == 0 else (MLA_V_DIM, 0)


def _rope(x, c, s1, s2):
    return (x * c + pltpu.roll(x, ROPE_HALF, 1) * s1
            + pltpu.roll(x, LANES - ROPE_HALF, 1) * s2)


def _inproj_kernel(*refs, pre_norm):
    refs = list(refs)
    x_ref = refs.pop(0)
    if pre_norm:
        lng_ref, lnb_ref = refs.pop(0), refs.pop(0)
    (w_ref, wg_ref, bg_ref, gq_ref, wuq_ref, gkv_ref, wuk_ref, wuv_ref, c_ref, s1_ref, s2_ref) = refs[:11]
    outs = refs[11:]
    if pre_norm:
        hn_ref = outs.pop(0)
        h = _layer_norm(x_ref[...], lng_ref[...], lnb_ref[...])
        hn_ref[...] = h
    else:
        h = x_ref[...]
    (ab_ref, ap_ref, du_ref, q_ref, k_ref, v_ref, gg_ref, laf_ref, lab_ref, qc_ref, kc_ref, vc_ref) = outs
    hb = h.astype(BF16)

    def proj(lo, n):
        return _dot(hb, w_ref[:, lo:lo + n])

    misc_ckv = proj(COL_MISC_CKV, 2 * LANES)
    cq = proj(COL_CQ, MLA_Q_RANK)
    misc = misc_ckv[:, :LANES]
    pre = _dot(misc.astype(BF16), wg_ref[...]) + bg_ref[...]
    la = _log_sigmoid(pre) * (1.0 / GLA_TAU)
    laf_ref[...] = la[:, :LANES]
    lab_ref[...] = la[:, LANES:]
    c, s1, s2 = c_ref[...], s1_ref[...], s2_ref[...]
    cqn = _rms_norm(cq, gq_ref[...]).astype(BF16)
    qm = _dot(cqn, wuq_ref[...])
    qscale = (MLA_NOPE + MLA_ROPE) ** -0.5 * LOG2E
    ckvn = _rms_norm(misc_ckv[:, LANES:], gkv_ref[...]).astype(BF16)
    kn = _dot(ckvn, wuk_ref[...])
    vn = _dot(ckvn, wuv_ref[...])
    lane = lax.broadcasted_iota(jnp.int32, misc.shape, 1)
    kr = _rope(jnp.where(lane >= ROPE_LO, misc, 0.0), c, s1, s2)
    for hh in range(MLA_HEADS):
        sl = slice(hh * HEAD_BLOCK, (hh + 1) * HEAD_BLOCK)
        qc_ref[:, sl] = (_rope(qm[:, sl], c, s1, s2) * qscale).astype(BF16)
        kc_ref[:, sl] = (kn[:, sl] + kr).astype(BF16)
        ones_col = jnp.where(lane == _value_lanes(hh)[1], 1.0, 0.0)
        vc_ref[:, sl] = (vn[:, sl] + ones_col).astype(BF16)
    ab_ref[...] = proj(COL_SC_B, W_GRP)
    ap_ref[...] = proj(COL_SC_C, W_GRP) * proj(COL_SC_H, W_GRP)
    du_ref[...] = proj(COL_CFM_A, W_GRP) * _sigmoid(proj(COL_CFM_G, W_GRP))
    qk = proj(COL_GQK, 2 * LANES)
    q_ref[...] = qk[:, :LANES] * (GLA_DK ** -0.5)
    k_ref[...] = qk[:, LANES:]
    v_ref[...] = proj(COL_GV, W_GRP)
    gg_ref[...] = proj(COL_GG, W_GRP)


def _inproj(h, lw, layer, rope, tm, input_norm=None):
    t = h.shape[0]
    f = lambda n, dt=F32: jax.ShapeDtypeStruct((t, n), dt)
    hb4 = MLA_HEADS * HEAD_BLOCK
    out_shape = [f(W_GRP), f(W_GRP), f(W_GRP), f(LANES), f(LANES), f(W_GRP), f(W_GRP), f(LANES), f(LANES),
                 f(hb4, BF16), f(hb4, BF16), f(hb4, BF16)]
    ls = functools.partial(_layer_spec, layer=layer)
    in_specs = [ls((D_MODEL, D_IN_PAD)), ls((LANES, 2 * LANES)), ls((1, 2 * LANES)),
                ls((1, MLA_Q_RANK)), ls((MLA_Q_RANK, hb4)),
                ls((1, MLA_KV_RANK)), ls((MLA_KV_RANK, hb4)), ls((MLA_KV_RANK, hb4)),
                _row_spec(tm, LANES), _row_spec(tm, LANES), _row_spec(tm, LANES)]
    args = [lw["w_in"], lw["w_gate"], lw["b_gate"], lw["g_mla_q"], lw["w_uq"], lw["g_mla_kv"], lw["w_uk"],
            lw["w_uv"], *rope]
    pre_norm = input_norm is not None
    if pre_norm:
        out_shape.insert(0, f(D_MODEL))
        in_specs = [_const_spec((1, D_MODEL)), _const_spec((1, D_MODEL))] + in_specs
        args = [a.reshape(1, D_MODEL) for a in input_norm] + args
    out_specs = [_row_spec(tm, s.shape[1]) for s in out_shape]
    return pl.pallas_call(
        functools.partial(_inproj_kernel, pre_norm=pre_norm), grid=(t // tm,),
        in_specs=[_row_spec(tm, D_MODEL)] + in_specs, out_specs=out_specs, out_shape=out_shape,
        compiler_params=_params("parallel"), name="inproj",
    )(h, *args)


def _gla_kernel(q_ref, k_ref, v_ref, laf_ref, lab_ref, gg_ref, ghead_ref, gain_ref, out_ref,
                acc_ref, sf_ref, sb_ref, *, seq):
    cs = GLA_CHUNK
    grp = min(GLA_GROUP, seq // cs)
    gr = grp * cs
    n_groups = seq // gr
    hk = GLA_HEADS * GLA_DK
    hv = GLA_HEADS * GLA_DV
    row = lax.broadcasted_iota(jnp.int32, (gr, gr), 0)
    col = lax.broadcasted_iota(jnp.int32, (gr, gr), 1)
    tri = jnp.where((row >= col) & (row // cs == col // cs), 1.0, 0.0).astype(BF16)
    srow = lax.broadcasted_iota(jnp.int32, (GLA_HEADS * cs, cs), 0) % cs
    scol = lax.broadcasted_iota(jnp.int32, (GLA_HEADS * cs, cs), 1)
    keep = (srow >= scol, srow < scol)
    qlane_head = lax.broadcasted_iota(jnp.int32, (cs, hk), 1) // GLA_DK
    olane_head = lax.broadcasted_iota(jnp.int32, (cs, hv), 1) // GLA_DV
    srow_head = lax.broadcasted_iota(jnp.int32, (hk, hv), 0) // GLA_DK
    scol_head = lax.broadcasted_iota(jnp.int32, (hk, hv), 1) // GLA_DV
    eye = (lax.broadcasted_iota(jnp.int32, (hk, hk), 0) == lax.broadcasted_iota(jnp.int32, (hk, hk), 1))

    acc_ref[...] = jnp.zeros_like(acc_ref)
    sf_ref[...] = jnp.zeros_like(sf_ref)
    sb_ref[...] = jnp.zeros_like(sb_ref)

    def decay_phase(g, direction):
        la_ref = (laf_ref, lab_ref)[direction]
        base = pl.multiple_of(g * gr, gr)
        la = la_ref[pl.ds(base, gr), :]
        hi = la.astype(BF16)
        rem = la - hi.astype(F32)
        mid = rem.astype(BF16)
        lo = (rem - mid.astype(F32)).astype(BF16)
        cum = _dot(tri, hi) + _dot(tri, mid) + _dot(tri, lo)
        return base, la, cum

    def chunk_phase(direction, base, la, cum):
        rows = pl.ds(base, gr)
        q, k = q_ref[rows, :], k_ref[rows, :]
        vb = v_ref[rows, :].astype(BF16)
        p, qin, vbs, kv, decay = [], [], [], [], []
        for i in range(grp):
            sl = slice(i * cs, (i + 1) * cs)
            total = cum[(i + 1) * cs - 1:(i + 1) * cs, :]
            b = cum[sl] if direction == 0 else total - cum[sl] + la[sl]
            ref = b[GLA_REF_ROW:GLA_REF_ROW + 1, :]
            qt = q[sl] * jnp.exp(b - ref)
            kt = k[sl] * jnp.exp(ref - b)
            qbd = jnp.concatenate([jnp.where(qlane_head == hh, qt, 0.0) for hh in range(GLA_HEADS)], axis=0)
            scores = _dot_nt(qbd.astype(BF16), kt.astype(BF16))
            p.append(jnp.where(keep[direction], scores, 0.0).astype(BF16))
            qin.append((qbd * jnp.exp(ref)).astype(BF16))
            khat = (kt * jnp.exp(total - ref)).astype(BF16)
            vbs.append(vb[sl])
            kv.append(jnp.where(srow_head == scol_head, _dot_tn(khat, vb[sl]), 0.0))
            decay.append(jnp.sum(jnp.where(eye, jnp.broadcast_to(jnp.exp(total), (hk, hk)), 0.0),
                                 axis=1, keepdims=True))
        return p, qin, vbs, kv, decay

    def state_phase(direction, base, p, qin, vbs, kv, decay):
        s_ref = (sf_ref, sb_ref)[direction]
        state = s_ref[...]
        for i in (range(grp) if direction == 0 else reversed(range(grp))):
            lhs = jnp.concatenate([qin[i], p[i]], axis=1)
            rhs = jnp.concatenate([state.astype(BF16), vbs[i]], axis=0)
            o_all = _dot(lhs, rhs)
            o = jnp.zeros((cs, hv), F32)
            for hh in range(GLA_HEADS):
                o = o + jnp.where(olane_head == hh, o_all[hh * cs:(hh + 1) * cs, :], 0.0)
            acc_ref[pl.ds(base + i * cs, cs), :] += o
            state = state * decay[i] + kv[i]
        s_ref[...] = state

    hrow = lax.broadcasted_iota(jnp.int32, (hv, hv), 0) // GLA_DV
    hcol = lax.broadcasted_iota(jnp.int32, (hv, hv), 1) // GLA_DV
    head_ones = jnp.where(hrow == hcol, 1.0, 0.0).astype(BF16)

    def finish(base):
        rows = pl.ds(base, gr)
        o = acc_ref[rows, :]
        sq = o * o
        hi = sq.astype(BF16)
        lo = (sq - hi.astype(F32)).astype(BF16)
        ms = (_dot(hi, head_ones) + _dot(lo, head_ones)) * (1.0 / GLA_DV)
        on = o * lax.rsqrt(ms + EPS) * ghead_ref[...]
        gg = gg_ref[rows, :]
        yb = on * (gg * _sigmoid(gg))
        out_ref[rows, :] = _rms_norm(yb, gain_ref[...]).astype(BF16)

    def finish_step(n):
        finish(pl.multiple_of(n * gr, gr))
        finish(pl.multiple_of((n_groups - 1 - n) * gr, gr))

    def step(n, finish_previous):
        if finish_previous:
            finish_step(n - 1)
        fwd = decay_phase(n, 0)
        bwd = decay_phase(n_groups - 1 - n, 1)
        fwd_chunks = chunk_phase(0, *fwd)
        bwd_chunks = chunk_phase(1, *bwd)
        state_phase(0, fwd[0], *fwd_chunks)
        state_phase(1, bwd[0], *bwd_chunks)

    assert n_groups % 2 == 0
    half = n_groups // 2
    lax.fori_loop(0, half + 1, lambda n, c: (step(n, False), c)[1], 0)
    lax.fori_loop(half + 1, n_groups, lambda n, c: (step(n, True), c)[1], 0)
    finish_step(n_groups - 1)


def _gla(q, k, v, laf, lab, gg, lw, layer, bsz, seq):
    hk, hv = GLA_HEADS * GLA_DK, GLA_HEADS * GLA_DV
    seq_spec = lambda n: pl.BlockSpec((seq, n), lambda b: (b, 0))
    return pl.pallas_call(
        functools.partial(_gla_kernel, seq=seq), grid=(bsz,),
        in_specs=[seq_spec(hk), seq_spec(hk), seq_spec(hv), seq_spec(hk), seq_spec(hk), seq_spec(hv),
                  _layer_spec((1, hv), layer), _layer_spec((1, W_GRP), layer, block=(0, GROUP_B))],
        out_specs=seq_spec(hv), out_shape=jax.ShapeDtypeStruct((bsz * seq, hv), BF16),
        scratch_shapes=[pltpu.VMEM((seq, hv), F32), pltpu.VMEM((hk, hv), F32), pltpu.VMEM((hk, hv), F32)],
        compiler_params=_params("parallel"), name="gla",
    )(q, k, v, laf, lab, gg, lw["g_gla_head"], lw["g_branch"])


def _attn_kernel(q_ref, k_ref, v_ref, gain_ref, out_ref, s0_ref, s1_ref, *, seq, tk):
    n_tiles = seq // tk
    bufs = (s0_ref, s1_ref)
    tiles = [(hh, j) for hh in range(MLA_HEADS) for j in range(n_tiles)]

    def scores(dst_ref, hh, j):
        sl = slice(hh * HEAD_BLOCK, (hh + 1) * HEAD_BLOCK)
        dst_ref[...] = _dot_nt(q_ref[:, sl], k_ref[j * tk:(j + 1) * tk, sl])

    heads = []
    m = acc = None
    scores(bufs[0], *tiles[0])
    for idx, (hh, j) in enumerate(tiles):
        if idx + 1 < len(tiles):
            scores(bufs[(idx + 1) % 2], *tiles[idx + 1])
        s = bufs[idx % 2][...]
        pv = lambda p: _dot(p.astype(BF16), v_ref[j * tk:(j + 1) * tk, hh * HEAD_BLOCK:(hh + 1) * HEAD_BLOCK])
        m_tile = jnp.max(s, axis=1, keepdims=True)
        if j == 0:
            m = m_tile
            acc = pv(jnp.exp2(s - m))
        else:
            m_new = jnp.maximum(m, m_tile)
            acc = acc * jnp.exp2(m - m_new) + pv(jnp.exp2(s - m_new))
            m = m_new
        if j == n_tiles - 1:
            ones_lane = _value_lanes(hh)[1]
            heads.append(acc / acc[:, ones_lane:ones_lane + 1])
    lane = lax.broadcasted_iota(jnp.int32, heads[0].shape, 1)
    low_half = lane < MLA_V_DIM
    y = jnp.concatenate([jnp.where(low_half, heads[hh], heads[hh + 1]) for hh in range(0, MLA_HEADS, 2)], axis=1)
    out_ref[...] = _rms_norm(y, gain_ref[...]).astype(BF16)


def _attention(qc, kc, vc, lw, layer, bsz, seq):
    tq, tk = min(ATTN_TQ, seq), min(ATTN_TK, seq)
    nq = seq // tq
    hb4 = MLA_HEADS * HEAD_BLOCK
    kv_spec = pl.BlockSpec((seq, hb4), lambda b, i: (b, 0))
    return pl.pallas_call(
        functools.partial(_attn_kernel, seq=seq, tk=tk), grid=(bsz, nq),
        in_specs=[pl.BlockSpec((tq, hb4), lambda b, i: (b * nq + i, 0)), kv_spec, kv_spec,
                  _layer_spec((1, W_GRP), layer, block=(0, GROUP_C))],
        out_specs=pl.BlockSpec((tq, W_GRP), lambda b, i: (b * nq + i, 0)),
        out_shape=jax.ShapeDtypeStruct((bsz * seq, W_GRP), BF16),
        scratch_shapes=[pltpu.VMEM((tq, tk), F32), pltpu.VMEM((tq, tk), F32)],
        compiler_params=_params("parallel", "parallel"), name="mla_attention",
    )(qc, kc, vc, lw["g_branch"])


def _outproj_kernel(ab_ref, ap_ref, app_ref, apn_ref, du_ref, dup_ref, dun_ref, yb_ref, yc_ref, h_ref,
                    wsc_ref, wcfm_ref, gcfm_ref, bcfm_ref, gain_ref, wout_ref, g1_ref, b1_ref,
                    out_ref, pext_ref, uext_ref, shift_ref, *, tiles_per_seq):
    tm = ab_ref.shape[0]
    i = pl.program_id(0) % tiles_per_seq
    not_first = (i != 0).astype(F32)
    not_last = (i != tiles_per_seq - 1).astype(F32)
    pext_ref[0:HALO, :] = app_ref[...] * not_first
    pext_ref[HALO:HALO + tm, :] = ap_ref[...]
    pext_ref[HALO + tm:, :] = apn_ref[...] * not_last
    uext_ref[0:HALO, :] = dup_ref[...] * not_first
    uext_ref[HALO:HALO + tm, :] = du_ref[...]
    uext_ref[HALO + tm:, :] = dun_ref[...] * not_last

    nb = OUTPROJ_ROW_BLOCKS
    rb = tm // nb
    first = HALO - CFM_WIDTH // 2
    span = SUBLANES * ((first + CFM_WIDTH - 1) // SUBLANES)
    assert shift_ref.shape[0] == tm + span and span + SUBLANES <= 2 * HALO
    u = [jnp.zeros((rb, W_GRP), F32) for _ in range(nb)]
    for r in range(SUBLANES):
        shift_ref[...] = uext_ref[pl.ds(r, tm + span), :]
        for off in range(0, span + 1, SUBLANES):
            j = off + r - first
            if 0 <= j < CFM_WIDTH:
                for blk in range(nb):
                    u[blk] = u[blk] + shift_ref[pl.ds(blk * rb + off, rb), :] * wcfm_ref[j:j + 1, :]

    for blk in range(nb):
        rows = pl.ds(blk * rb, rb)
        conv = jnp.zeros((rb, W_GRP), F32)
        for j in range(SC_WIDTH):
            conv = conv + pext_ref[pl.ds(blk * rb + HALO - SC_WIDTH // 2 + j, rb), :] * wsc_ref[j:j + 1, :]
        ya = _rms_norm(ab_ref[rows, :] * conv, gain_ref[:, GROUP_A * W_GRP:(GROUP_A + 1) * W_GRP])
        un = _layer_norm(u[blk], gcfm_ref[...], bcfm_ref[...])
        yd = _rms_norm(un * _sigmoid(un), gain_ref[:, GROUP_D * W_GRP:(GROUP_D + 1) * W_GRP])
        mix = (_dot(ya.astype(BF16), wout_ref[0:W_GRP, :])
               + _dot(yb_ref[rows, :], wout_ref[W_GRP:2 * W_GRP, :])
               + _dot(yc_ref[rows, :], wout_ref[2 * W_GRP:3 * W_GRP, :])
               + _dot(yd.astype(BF16), wout_ref[3 * W_GRP:4 * W_GRP, :]))
        out_ref[rows, :] = _layer_norm(DN_ALPHA * h_ref[rows, :] + mix, g1_ref[...], b1_ref[...])


def _halo_specs(tm, cols, halo, total_rows):
    per = tm // halo
    last = total_rows // halo - 1
    prev = pl.BlockSpec((halo, cols), lambda i: (jnp.maximum(i * per - 1, 0), 0))
    nxt = pl.BlockSpec((halo, cols), lambda i: (jnp.minimum((i + 1) * per, last), 0))
    return prev, nxt


def _outproj(ab, ap, du, yb, yc, h, lw, layer, seq, tm):
    t = h.shape[0]
    prev, nxt = _halo_specs(tm, W_GRP, HALO, t)
    g = _row_spec(tm, W_GRP)
    ls = functools.partial(_layer_spec, layer=layer)
    in_specs = [g, g, prev, nxt, g, prev, nxt, g, g, _row_spec(tm, D_MODEL),
                ls((SC_WIDTH, W_GRP)), ls((CFM_WIDTH, W_GRP)), ls((1, W_GRP)), ls((1, W_GRP)),
                ls((1, 4 * W_GRP)), ls((4 * W_GRP, D_MODEL)), ls((1, D_MODEL)), ls((1, D_MODEL))]
    return pl.pallas_call(
        functools.partial(_outproj_kernel, tiles_per_seq=seq // tm), grid=(t // tm,),
        in_specs=in_specs, out_specs=_row_spec(tm, D_MODEL),
        out_shape=jax.ShapeDtypeStruct((t, D_MODEL), F32),
        scratch_shapes=[pltpu.VMEM((tm + 2 * HALO, W_GRP), F32), pltpu.VMEM((tm + 2 * HALO, W_GRP), F32),
                        pltpu.VMEM((tm + 2 * HALO - SUBLANES, W_GRP), F32)],
        compiler_params=_params("parallel"), name="outproj",
    )(ab, ap, ap, ap, du, du, du, yb, yc, h, lw["w_sc_conv"], lw["w_cfm_dw"], lw["g_cfm_ln"],
      lw["b_cfm_ln"], lw["g_branch"], lw["w_out"], lw["ln1_g"], lw["ln1_b"])


def _ffn_kernel(h_ref, hp_ref, hn_ref, wup_ref, wdw_ref, wdown_ref, g2_ref, b2_ref, out_ref,
                hx_ref, act_ref, ug0_ref, uv0_ref, ug1_ref, uv1_ref, *, tiles_per_seq, n_chunks):
    tm = h_ref.shape[0]
    i = pl.program_id(0) % tiles_per_seq
    not_first = (i != 0).astype(F32)
    not_last = (i != tiles_per_seq - 1).astype(F32)
    ext = tm + 2 * FFN_HALO
    hx_ref[0:FFN_HALO, :] = (hp_ref[...] * not_first).astype(BF16)
    hx_ref[FFN_HALO:FFN_HALO + tm, :] = h_ref[...].astype(BF16)
    hx_ref[FFN_HALO + tm:, :] = (hn_ref[...] * not_last).astype(BF16)

    def conv3(u, w):
        prev = pltpu.roll(u, 1, 0)
        nxt = pltpu.roll(u, ext - 1, 0)
        full = prev * w[0:1, :] + u * w[1:2, :] + nxt * w[2:3, :]
        return full[FFN_HALO:FFN_HALO + tm, :]

    bufs = ((ug0_ref, uv0_ref), (ug1_ref, uv1_ref))

    def up_cols(c, half):
        return wup_ref[:, pl.ds(half * D_FF + c * FF_CHUNK, FF_CHUNK)]

    def up(slot, c):
        hx = hx_ref[...]
        bufs[slot][0][...] = _dot(hx, up_cols(c, 0))
        bufs[slot][1][...] = _dot(hx, up_cols(c, 1))

    def activation(slot, c):
        gate = conv3(bufs[slot][0][...], wdw_ref[0, c])
        val = conv3(bufs[slot][1][...], wdw_ref[1, c])
        return (gate * _sigmoid(gate) * val).astype(BF16)

    up(0, 0)
    for c in range(n_chunks):
        if c + 1 < n_chunks:
            up((c + 1) % 2, c + 1)
        act_ref[:, c * FF_CHUNK:(c + 1) * FF_CHUNK] = activation(c % 2, c)
    rb = tm // FFN_OUT_BLOCKS
    for blk in range(FFN_OUT_BLOCKS):
        rows = pl.ds(blk * rb, rb)
        mix = _dot(act_ref[rows, :], wdown_ref[...])
        out_ref[rows, :] = _layer_norm(DN_ALPHA * h_ref[rows, :] + mix, g2_ref[...], b2_ref[...])


def _ffn(h, lw, layer, seq, tm):
    t = h.shape[0]
    n_chunks = D_FF // FF_CHUNK
    prev, nxt = _halo_specs(tm, D_MODEL, FFN_HALO, t)
    ls = functools.partial(_layer_spec, layer=layer)
    in_specs = [_row_spec(tm, D_MODEL), prev, nxt,
                ls((D_MODEL, 2 * D_FF)), ls((2, n_chunks, FFN_CONV_WIDTH, FF_CHUNK)),
                ls((D_FF, D_MODEL)), ls((1, D_MODEL)), ls((1, D_MODEL))]
    ext = tm + 2 * FFN_HALO
    return pl.pallas_call(
        functools.partial(_ffn_kernel, tiles_per_seq=seq // tm, n_chunks=n_chunks), grid=(t // tm,),
        in_specs=in_specs, out_specs=_row_spec(tm, D_MODEL),
        out_shape=jax.ShapeDtypeStruct((t, D_MODEL), F32),
        scratch_shapes=[pltpu.VMEM((ext, D_MODEL), BF16), pltpu.VMEM((tm, D_FF), BF16)]
        + [pltpu.VMEM((ext, FF_CHUNK), F32)] * 4,
        compiler_params=_params("parallel"), name="conv_ffn",
    )(h, h, h, lw["w_ffn_up"], lw["w_ffn_dw"], lw["w_ffn_down"], lw["ln2_g"], lw["ln2_b"])


def _head_blocks(w, per_head, lo, n, value_layout=False):
    depth, rows = w.shape[:2]
    wh = w.reshape(depth, rows, -1, per_head)[..., lo:lo + n]
    blocks = jnp.pad(wh, ((0, 0), (0, 0), (0, 0), (0, HEAD_BLOCK - n)))
    if value_layout:
        shifted = jnp.pad(wh, ((0, 0), (0, 0), (0, 0), (HEAD_BLOCK - n, 0)))
        odd = (jnp.arange(wh.shape[2]) % 2 == 1)[None, None, :, None]
        blocks = jnp.where(odd, shifted, blocks)
    return blocks.reshape(depth, rows, -1)


def _stack_weights(p):
    w_in = p["w_in"]
    depth = w_in.shape[0]
    sizes = (256, 256, 256, 128, 128, 256, 256, 16, 16, 256, 128, 32, 512)
    offs = [0]
    for s in sizes:
        offs.append(offs[-1] + s)
    seg = lambda idx: w_in[:, :, offs[idx]:offs[idx + 1]]
    zeros = lambda n: jnp.zeros((depth, D_MODEL, n), F32)
    misc = [seg(7), seg(8), zeros(ROPE_LO - 2 * GLA_LOWRANK), seg(11), zeros(LANES - ROPE_LO - MLA_ROPE)]
    w_in_p = jnp.concatenate([seg(0), seg(1), seg(2), seg(3), seg(4), seg(5), seg(6), *misc, seg(10), seg(9),
                              seg(12)], axis=2).astype(BF16)
    hk = GLA_HEADS * GLA_DK
    up = p["w_gla_a_up"]
    zk = jnp.zeros((depth, GLA_LOWRANK, hk), F32)
    w_gate = jnp.concatenate([jnp.concatenate([up[:, 0], zk], axis=2), jnp.concatenate([zk, up[:, 1]], axis=2),
                              jnp.zeros((depth, LANES - 2 * GLA_LOWRANK, 2 * hk), F32)], axis=1)
    w_ukv = p["w_mla_ukv"]
    n_ff = D_FF // FF_CHUNK
    w_dw = p["w_ffn_dw"].reshape(depth, FFN_CONV_WIDTH, 2, n_ff, FF_CHUNK).transpose(0, 2, 3, 1, 4)
    row = lambda a: a.reshape(depth, 1, -1)
    return {
        "w_in": w_in_p,
        "w_gate": w_gate.astype(BF16),
        "b_gate": row(p["b_gla_a"]),
        "g_mla_q": row(p["g_mla_q"]),
        "w_uq": _head_blocks(p["w_mla_uq"], MLA_NOPE + MLA_ROPE, 0, MLA_NOPE + MLA_ROPE).astype(BF16),
        "g_mla_kv": row(p["g_mla_kv"]),
        "w_uk": _head_blocks(w_ukv, MLA_NOPE + MLA_V_DIM, 0, MLA_NOPE).astype(BF16),
        "w_uv": _head_blocks(w_ukv, MLA_NOPE + MLA_V_DIM, MLA_NOPE, MLA_V_DIM, value_layout=True).astype(BF16),
        "g_gla_head": row(p["g_gla_head"]),
        "w_sc_conv": p["w_sc_conv"],
        "w_cfm_dw": p["w_cfm_dw"],
        "g_cfm_ln": row(p["g_cfm_ln"]),
        "b_cfm_ln": row(p["b_cfm_ln"]),
        "g_branch": row(p["g_branch"]),
        "w_out": p["w_out"].astype(BF16),
        "ln1_g": row(p["ln1_g"]),
        "ln1_b": row(p["ln1_b"]),
        "w_ffn_up": p["w_ffn_up"].astype(BF16),
        "w_ffn_dw": w_dw,
        "w_ffn_down": p["w_ffn_down"].astype(BF16),
        "ln2_g": row(p["ln2_g"]),
        "ln2_b": row(p["ln2_b"]),
    }


def kernel(x, positions, ln_in_g, ln_in_b, w_in, w_sc_conv, w_gla_a_up, b_gla_a, g_gla_head, g_mla_q, w_mla_uq,
           g_mla_kv, w_mla_ukv, w_cfm_dw, g_cfm_ln, b_cfm_ln, g_branch, w_out, ln1_g, ln1_b, w_ffn_up, w_ffn_dw,
           w_ffn_down, ln2_g, ln2_b):
    p = dict(w_in=w_in, w_sc_conv=w_sc_conv, w_gla_a_up=w_gla_a_up, b_gla_a=b_gla_a, g_gla_head=g_gla_head,
             g_mla_q=g_mla_q, w_mla_uq=w_mla_uq, g_mla_kv=g_mla_kv, w_mla_ukv=w_mla_ukv, w_cfm_dw=w_cfm_dw,
             g_cfm_ln=g_cfm_ln, b_cfm_ln=b_cfm_ln, g_branch=g_branch, w_out=w_out, ln1_g=ln1_g, ln1_b=ln1_b,
             w_ffn_up=w_ffn_up, w_ffn_dw=w_ffn_dw, w_ffn_down=w_ffn_down, ln2_g=ln2_g, ln2_b=ln2_b)
    bsz, seq, d = x.shape
    assert d == D_MODEL and seq % GLA_CHUNK == 0
    tm = min(TOKEN_TILE, seq)
    assert seq % tm == 0 and tm % HALO == 0
    t = bsz * seq
    rope = _rope_tables(positions, tm)
    h = x.reshape(t, d)
    lw = _stack_weights(p)
    for l in range(DEPTH):
        if l == 0:
            h, *mixer_in = _inproj(h, lw, l, rope, tm, input_norm=(ln_in_g, ln_in_b))
        else:
            mixer_in = _inproj(h, lw, l, rope, tm)
        ab, ap, du, q, k, v, gg, laf, lab, qc, kc, vc = mixer_in
        yb = _gla(q, k, v, laf, lab, gg, lw, l, bsz, seq)
        yc = _attention(qc, kc, vc, lw, l, bsz, seq)
        h = _outproj(ab, ap, du, yb, yc, h, lw, l, seq, tm)
        h = _ffn(h, lw, l, seq, min(FFN_TILE, seq))
    return h.reshape(bsz, seq, d)
```

```python
import functools
import math

import jax
import jax.numpy as jnp
from jax import lax
from jax.experimental import pallas as pl
from jax.experimental.pallas import tpu as pltpu

F32 = jnp.float32
BF16 = jnp.bfloat16

D_MODEL = 1024
DEPTH = 2
W_GRP = 256
GROUP_A, GROUP_B, GROUP_C, GROUP_D = range(4)
SC_WIDTH = 3
GLA_HEADS = 4
GLA_DK = 32
GLA_DV = 64
GLA_LOWRANK = 16
GLA_TAU = 16.0
MLA_HEADS = 4
MLA_V_DIM = 64
MLA_NOPE = 64
MLA_ROPE = 32
MLA_Q_RANK = 256
MLA_KV_RANK = 128
ROPE_BASE = 10000.0
CFM_WIDTH = 31
D_FF = 2816
FFN_CONV_WIDTH = 3
DN_ALPHA = (2.0 * DEPTH) ** 0.25
EPS = 1e-5

LANES = 128
SUBLANES = 8
VMEM_LIMIT_BYTES = 56 * 1024 * 1024

TOKEN_TILE = 512
FFN_TILE = 1024
FFN_OUT_BLOCKS = 4
HALO = 16
FFN_HALO = SUBLANES
FF_CHUNK = 256
GLA_CHUNK = 64
GLA_REF_ROW = GLA_CHUNK // 2
GLA_GROUP = 4
ATTN_TQ = 512
ATTN_TK = 2048
HEAD_BLOCK = LANES
ROPE_LO = MLA_NOPE
ROPE_HALF = MLA_ROPE // 2
LOG2E = math.log2(math.e)

COL_SC_B, COL_SC_C, COL_SC_H = 0, 256, 512
COL_GQK, COL_GV, COL_GG = 768, 1024, 1280
COL_MISC_CKV = 1536
COL_CQ = 1792
COL_CFM_A, COL_CFM_G = 2048, 2304
D_IN_PAD = 2560


def _params(*sem):
    return pltpu.CompilerParams(dimension_semantics=sem, vmem_limit_bytes=VMEM_LIMIT_BYTES)


def _dot(a, b):
    return jnp.dot(a, b, preferred_element_type=F32)


def _dot_nt(a, b):
    return lax.dot_general(a, b, (((1,), (1,)), ((), ())), preferred_element_type=F32)


def _dot_tn(a, b):
    return lax.dot_general(a, b, (((0,), (0,)), ((), ())), preferred_element_type=F32)


def _layer_norm(x, g, b):
    mu = jnp.mean(x, axis=-1, keepdims=True)
    xc = x - mu
    var = jnp.mean(xc * xc, axis=-1, keepdims=True)
    return xc * lax.rsqrt(var + EPS) * g + b


def _rms_norm(x, g):
    return x * lax.rsqrt(jnp.mean(x * x, axis=-1, keepdims=True) + EPS) * g


def _sigmoid(x):
    return 1.0 / (1.0 + jnp.exp(-x))


def _log_sigmoid(x):
    return jnp.minimum(x, 0.0) - jnp.log1p(jnp.exp(-jnp.abs(x)))


def _const_spec(shape):
    return pl.BlockSpec(shape, lambda *_: (0,) * len(shape), pipeline_mode=pl.Buffered(1))


def _layer_spec(shape, layer, block=None):
    idx = (layer,) + (tuple(block) if block is not None else (0,) * len(shape))
    return pl.BlockSpec((None,) + tuple(shape), lambda *_: idx, pipeline_mode=pl.Buffered(1))


def _row_spec(rows, cols):
    return pl.BlockSpec((rows, cols), lambda i: (i, 0))


def _rope_kernel(pos_ref, inv_ref, c_ref, s1_ref, s2_ref):
    ang = inv_ref[...] * pos_ref[...].astype(F32)
    frow = lax.broadcasted_iota(jnp.int32, (ROPE_HALF, LANES), 0)
    lane = lax.broadcasted_iota(jnp.int32, (ROPE_HALF, LANES), 1)
    to_first = jnp.where(lane == ROPE_LO + frow, 1.0, 0.0).astype(BF16)
    to_second = jnp.where(lane == ROPE_LO + ROPE_HALF + frow, 1.0, 0.0).astype(BF16)

    def place(x, where_to):
        hi = x.astype(BF16)
        rem = x - hi.astype(F32)
        mid = rem.astype(BF16)
        lo = (rem - mid.astype(F32)).astype(BF16)
        return _dot_tn(hi, where_to) + _dot_tn(mid, where_to) + _dot_tn(lo, where_to)

    cos, sin = jnp.cos(ang), jnp.sin(ang)
    out_lane = lax.broadcasted_iota(jnp.int32, c_ref.shape, 1)
    c_ref[...] = jnp.where(out_lane < ROPE_LO, 1.0, place(cos, to_first + to_second))
    s1_ref[...] = place(sin, to_second)
    s2_ref[...] = -place(sin, to_first)


def _rope_tables(positions, tm):
    t = positions.size
    inv = ROPE_BASE ** (-jnp.arange(0, MLA_ROPE, 2, dtype=F32) / MLA_ROPE)
    out = jax.ShapeDtypeStruct((t, LANES), F32)
    return pl.pallas_call(
        _rope_kernel, grid=(t // tm,),
        in_specs=[pl.BlockSpec((1, tm), lambda i: (0, i)), _const_spec((ROPE_HALF, 1))],
        out_specs=[_row_spec(tm, LANES)] * 3, out_shape=[out] * 3,
        compiler_params=_params("parallel"), name="rope_tables",
    )(positions.reshape(1, t), inv.reshape(ROPE_HALF, 1))


def _value_lanes(head):
    assert 2 * MLA_V_DIM == HEAD_BLOCK
    return (0, MLA_V_DIM) if head % 2 == 0 else (MLA_V_DIM, 0)


def _rope(x, c, s1, s2):
    return (x * c + pltpu.roll(x, ROPE_HALF, 1) * s1
            + pltpu.roll(x, LANES - ROPE_HALF, 1) * s2)


def _inproj_kernel(*refs, pre_norm):
    refs = list(refs)
    x_ref = refs.pop(0)
    if pre_norm:
        lng_ref, lnb_ref = refs.pop(0), refs.pop(0)
    (w_ref, wg_ref, bg_ref, gq_ref, wuq_ref, gkv_ref, wuk_ref, wuv_ref, c_ref, s1_ref, s2_ref) = refs[:11]
    outs = refs[11:]
    if pre_norm:
        hn_ref = outs.pop(0)
        h = _layer_norm(x_ref[...], lng_ref[...], lnb_ref[...])
        hn_ref[...] = h
    else:
        h = x_ref[...]
    (ab_ref, ap_ref, du_ref, q_ref, k_ref, v_ref, gg_ref, laf_ref, lab_ref, qc_ref, kc_ref, vc_ref) = outs
    hb = h.astype(BF16)

    def proj(lo, n):
        return _dot(hb, w_ref[:, lo:lo + n])

    misc_ckv = proj(COL_MISC_CKV, 2 * LANES)
    cq = proj(COL_CQ, MLA_Q_RANK)
    misc = misc_ckv[:, :LANES]
    pre = _dot(misc.astype(BF16), wg_ref[...]) + bg_ref[...]
    la = _log_sigmoid(pre) * (1.0 / GLA_TAU)
    laf_ref[...] = la[:, :LANES]
    lab_ref[...] = la[:, LANES:]
    c, s1, s2 = c_ref[...], s1_ref[...], s2_ref[...]
    cqn = _rms_norm(cq, gq_ref[...]).astype(BF16)
    qm = _dot(cqn, wuq_ref[...])
    qscale = (MLA_NOPE + MLA_ROPE) ** -0.5 * LOG2E
    ckvn = _rms_norm(misc_ckv[:, LANES:], gkv_ref[...]).astype(BF16)
    kn = _dot(ckvn, wuk_ref[...])
    vn = _dot(ckvn, wuv_ref[...])
    lane = lax.broadcasted_iota(jnp.int32, misc.shape, 1)
    kr = _rope(jnp.where(lane >= ROPE_LO, misc, 0.0), c, s1, s2)
    for hh in range(MLA_HEADS):
        sl = slice(hh * HEAD_BLOCK, (hh + 1) * HEAD_BLOCK)
        qc_ref[:, sl] = (_rope(qm[:, sl], c, s1, s2) * qscale).astype(BF16)
        kc_ref[:, sl] = (kn[:, sl] + kr).astype(BF16)
        ones_col = jnp.where(lane == _value_lanes(hh)[1], 1.0, 0.0)
        vc_ref[:, sl] = (vn[:, sl] + ones_col).astype(BF16)
    ab_ref[...] = proj(COL_SC_B, W_GRP)
    ap_ref[...] = proj(COL_SC_C, W_GRP) * proj(COL_SC_H, W_GRP)
    du_ref[...] = proj(COL_CFM_A, W_GRP) * _sigmoid(proj(COL_CFM_G, W_GRP))
    qk = proj(COL_GQK, 2 * LANES)
    q_ref[...] = qk[:, :LANES] * (GLA_DK ** -0.5)
    k_ref[...] = qk[:, LANES:]
    v_ref[...] = proj(COL_GV, W_GRP)
    gg_ref[...] = proj(COL_GG, W_GRP)


def _inproj(h, lw, layer, rope, tm, input_norm=None):
    t = h.shape[0]
    f = lambda n, dt=F32: jax.ShapeDtypeStruct((t, n), dt)
    hb4 = MLA_HEADS * HEAD_BLOCK
    out_shape = [f(W_GRP), f(W_GRP), f(W_GRP), f(LANES), f(LANES), f(W_GRP), f(W_GRP), f(LANES), f(LANES),
                 f(hb4, BF16), f(hb4, BF16), f(hb4, BF16)]
    ls = functools.partial(_layer_spec, layer=layer)
    in_specs = [ls((D_MODEL, D_IN_PAD)), ls((LANES, 2 * LANES)), ls((1, 2 * LANES)),
                ls((1, MLA_Q_RANK)), ls((MLA_Q_RANK, hb4)),
                ls((1, MLA_KV_RANK)), ls((MLA_KV_RANK, hb4)), ls((MLA_KV_RANK, hb4)),
                _row_spec(tm, LANES), _row_spec(tm, LANES), _row_spec(tm, LANES)]
    args = [lw["w_in"], lw["w_gate"], lw["b_gate"], lw["g_mla_q"], lw["w_uq"], lw["g_mla_kv"], lw["w_uk"],
            lw["w_uv"], *rope]
    pre_norm = input_norm is not None
    if pre_norm:
        out_shape.insert(0, f(D_MODEL))
        in_specs = [_const_spec((1, D_MODEL)), _const_spec((1, D_MODEL))] + in_specs
        args = [a.reshape(1, D_MODEL) for a in input_norm] + args
    out_specs = [_row_spec(tm, s.shape[1]) for s in out_shape]
    return pl.pallas_call(
        functools.partial(_inproj_kernel, pre_norm=pre_norm), grid=(t // tm,),
        in_specs=[_row_spec(tm, D_MODEL)] + in_specs, out_specs=out_specs, out_shape=out_shape,
        compiler_params=_params("parallel"), name="inproj",
    )(h, *args)


def _gla_kernel(q_ref, k_ref, v_ref, laf_ref, lab_ref, gg_ref, ghead_ref, gain_ref, out_ref,
                acc_ref, sf_ref, sb_ref, *, seq):
    cs = GLA_CHUNK
    grp = min(GLA_GROUP, seq // cs)
    gr = grp * cs
    n_groups = seq // gr
    hk = GLA_HEADS * GLA_DK
    hv = GLA_HEADS * GLA_DV
    row = lax.broadcasted_iota(jnp.int32, (gr, gr), 0)
    col = lax.broadcasted_iota(jnp.int32, (gr, gr), 1)
    tri = jnp.where((row >= col) & (row // cs == col // cs), 1.0, 0.0).astype(BF16)
    srow = lax.broadcasted_iota(jnp.int32, (GLA_HEADS * cs, cs), 0) % cs
    scol = lax.broadcasted_iota(jnp.int32, (GLA_HEADS * cs, cs), 1)
    keep = (srow >= scol, srow < scol)
    qlane_head = lax.broadcasted_iota(jnp.int32, (cs, hk), 1) // GLA_DK
    olane_head = lax.broadcasted_iota(jnp.int32, (cs, hv), 1) // GLA_DV
    srow_head = lax.broadcasted_iota(jnp.int32, (hk, hv), 0) // GLA_DK
    scol_head = lax.broadcasted_iota(jnp.int32, (hk, hv), 1) // GLA_DV
    eye = (lax.broadcasted_iota(jnp.int32, (hk, hk), 0) == lax.broadcasted_iota(jnp.int32, (hk, hk), 1))

    acc_ref[...] = jnp.zeros_like(acc_ref)
    sf_ref[...] = jnp.zeros_like(sf_ref)
    sb_ref[...] = jnp.zeros_like(sb_ref)

    def decay_phase(g, direction):
        la_ref = (laf_ref, lab_ref)[direction]
        base = pl.multiple_of(g * gr, gr)
        la = la_ref[pl.ds(base, gr), :]
        hi = la.astype(BF16)
        rem = la - hi.astype(F32)
        mid = rem.astype(BF16)
        lo = (rem - mid.astype(F32)).astype(BF16)
        cum = _dot(tri, hi) + _dot(tri, mid) + _dot(tri, lo)
        return base, la, cum

    def chunk_phase(direction, base, la, cum):
        rows = pl.ds(base, gr)
        q, k = q_ref[rows, :], k_ref[rows, :]
        vb = v_ref[rows, :].astype(BF16)
        p, qin, vbs, kv, decay = [], [], [], [], []
        for i in range(grp):
            sl = slice(i * cs, (i + 1) * cs)
            total = cum[(i + 1) * cs - 1:(i + 1) * cs, :]
            b = cum[sl] if direction == 0 else total - cum[sl] + la[sl]
            ref = b[GLA_REF_ROW:GLA_REF_ROW + 1, :]
            qt = q[sl] * jnp.exp(b - ref)
            kt = k[sl] * jnp.exp(ref - b)
            qbd = jnp.concatenate([jnp.where(qlane_head == hh, qt, 0.0) for hh in range(GLA_HEADS)], axis=0)
            scores = _dot_nt(qbd.astype(BF16), kt.astype(BF16))
            p.append(jnp.where(keep[direction], scores, 0.0).astype(BF16))
            qin.append((qbd * jnp.exp(ref)).astype(BF16))
            khat = (kt * jnp.exp(total - ref)).astype(BF16)
            vbs.append(vb[sl])
            kv.append(jnp.where(srow_head == scol_head, _dot_tn(khat, vb[sl]), 0.0))
            decay.append(jnp.sum(jnp.where(eye, jnp.broadcast_to(jnp.exp(total), (hk, hk)), 0.0),
                                 axis=1, keepdims=True))
        return p, qin, vbs, kv, decay

    def state_phase(direction, base, p, qin, vbs, kv, decay):
        s_ref = (sf_ref, sb_ref)[direction]
        state = s_ref[...]
        for i in (range(grp) if direction == 0 else reversed(range(grp))):
            lhs = jnp.concatenate([qin[i], p[i]], axis=1)
            rhs = jnp.concatenate([state.astype(BF16), vbs[i]], axis=0)
            o_all = _dot(lhs, rhs)
            o = jnp.zeros((cs, hv), F32)
            for hh in range(GLA_HEADS):
                o = o + jnp.where(olane_head == hh, o_all[hh * cs:(hh + 1) * cs, :], 0.0)
            acc_ref[pl.ds(base + i * cs, cs), :] += o
            state = state * decay[i] + kv[i]
        s_ref[...] = state

    hrow = lax.broadcasted_iota(jnp.int32, (hv, hv), 0) // GLA_DV
    hcol = lax.broadcasted_iota(jnp.int32, (hv, hv), 1) // GLA_DV
    head_ones = jnp.where(hrow == hcol, 1.0, 0.0).astype(BF16)

    def finish(base):
        rows = pl.ds(base, gr)
        o = acc_ref[rows, :]
        sq = o * o
        hi = sq.astype(BF16)
        lo = (sq - hi.astype(F32)).astype(BF16)
        ms = (_dot(hi, head_ones) + _dot(lo, head_ones)) * (1.0 / GLA_DV)
        on = o * lax.rsqrt(ms + EPS) * ghead_ref[...]
        gg = gg_ref[rows, :]
        yb = on * (gg * _sigmoid(gg))
        out_ref[rows, :] = _rms_norm(yb, gain_ref[...]).astype(BF16)

    def finish_step(n):
        finish(pl.multiple_of(n * gr, gr))
        finish(pl.multiple_of((n_groups - 1 - n) * gr, gr))

    def step(n, finish_previous):
        if finish_previous:
            finish_step(n - 1)
        fwd = decay_phase(n, 0)
        bwd = decay_phase(n_groups - 1 - n, 1)
        fwd_chunks = chunk_phase(0, *fwd)
        bwd_chunks = chunk_phase(1, *bwd)
        state_phase(0, fwd[0], *fwd_chunks)
        state_phase(1, bwd[0], *bwd_chunks)

    assert n_groups % 2 == 0
    half = n_groups // 2
    lax.fori_loop(0, half + 1, lambda n, c: (step(n, False), c)[1], 0)
    lax.fori_loop(half + 1, n_groups, lambda n, c: (step(n, True), c)[1], 0)
    finish_step(n_groups - 1)


def _gla(q, k, v, laf, lab, gg, lw, layer, bsz, seq):
    hk, hv = GLA_HEADS * GLA_DK, GLA_HEADS * GLA_DV
    seq_spec = lambda n: pl.BlockSpec((seq, n), lambda b: (b, 0))
    return pl.pallas_call(
        functools.partial(_gla_kernel, seq=seq), grid=(bsz,),
        in_specs=[seq_spec(hk), seq_spec(hk), seq_spec(hv), seq_spec(hk), seq_spec(hk), seq_spec(hv),
                  _layer_spec((1, hv), layer), _layer_spec((1, W_GRP), layer, block=(0, GROUP_B))],
        out_specs=seq_spec(hv), out_shape=jax.ShapeDtypeStruct((bsz * seq, hv), BF16),
        scratch_shapes=[pltpu.VMEM((seq, hv), F32), pltpu.VMEM((hk, hv), F32), pltpu.VMEM((hk, hv), F32)],
        compiler_params=_params("parallel"), name="gla",
    )(q, k, v, laf, lab, gg, lw["g_gla_head"], lw["g_branch"])


def _attn_kernel(q_ref, k_ref, v_ref, gain_ref, out_ref, s0_ref, s1_ref, *, seq, tk):
    n_tiles = seq // tk
    bufs = (s0_ref, s1_ref)
    tiles = [(hh, j) for hh in range(MLA_HEADS) for j in range(n_tiles)]

    def scores(dst_ref, hh, j):
        sl = slice(hh * HEAD_BLOCK, (hh + 1) * HEAD_BLOCK)
        dst_ref[...] = _dot_nt(q_ref[:, sl], k_ref[j * tk:(j + 1) * tk, sl])

    heads = []
    m = acc = None
    scores(bufs[0], *tiles[0])
    for idx, (hh, j) in enumerate(tiles):
        if idx + 1 < len(tiles):
            scores(bufs[(idx + 1) % 2], *tiles[idx + 1])
        s = bufs[idx % 2][...]
        pv = lambda p: _dot(p.astype(BF16), v_ref[j * tk:(j + 1) * tk, hh * HEAD_BLOCK:(hh + 1) * HEAD_BLOCK])
        m_tile = jnp.max(s, axis=1, keepdims=True)
        if j == 0:
            m = m_tile
            acc = pv(jnp.exp2(s - m))
        else:
            m_new = jnp.maximum(m, m_tile)
            acc = acc * jnp.exp2(m - m_new) + pv(jnp.exp2(s - m_new))
            m = m_new
        if j == n_tiles - 1:
            ones_lane = _value_lanes(hh)[1]
            heads.append(acc / acc[:, ones_lane:ones_lane + 1])
    lane = lax.broadcasted_iota(jnp.int32, heads[0].shape, 1)
    low_half = lane < MLA_V_DIM
    y = jnp.concatenate([jnp.where(low_half, heads[hh], heads[hh + 1]) for hh in range(0, MLA_HEADS, 2)], axis=1)
    out_ref[...] = _rms_norm(y, gain_ref[...]).astype(BF16)


def _attention(qc, kc, vc, lw, layer, bsz, seq):
    tq, tk = min(ATTN_TQ, seq), min(ATTN_TK, seq)
    nq = seq // tq
    hb4 = MLA_HEADS * HEAD_BLOCK
    kv_spec = pl.BlockSpec((seq, hb4), lambda b, i: (b, 0))
    return pl.pallas_call(
        functools.partial(_attn_kernel, seq=seq, tk=tk), grid=(bsz, nq),
        in_specs=[pl.BlockSpec((tq, hb4), lambda b, i: (b * nq + i, 0)), kv_spec, kv_spec,
                  _layer_spec((1, W_GRP), layer, block=(0, GROUP_C))],
        out_specs=pl.BlockSpec((tq, W_GRP), lambda b, i: (b * nq + i, 0)),
        out_shape=jax.ShapeDtypeStruct((bsz * seq, W_GRP), BF16),
        scratch_shapes=[pltpu.VMEM((tq, tk), F32), pltpu.VMEM((tq, tk), F32)],
        compiler_params=_params("parallel", "parallel"), name="mla_attention",
    )(qc, kc, vc, lw["g_branch"])


def _outproj_kernel(ab_ref, ap_ref, app_ref, apn_ref, du_ref, dup_ref, dun_ref, yb_ref, yc_ref, h_ref,
                    wsc_ref, wcfm_ref, gcfm_ref, bcfm_ref, gain_ref, wout_ref, g1_ref, b1_ref,
                    out_ref, pext_ref, uext_ref, shift_ref, *, tiles_per_seq):
    tm = ab_ref.shape[0]
    i = pl.program_id(0) % tiles_per_seq
    not_first = (i != 0).astype(F32)
    not_last = (i != tiles_per_seq - 1).astype(F32)
    pext_ref[0:HALO, :] = app_ref[...] * not_first
    pext_ref[HALO:HALO + tm, :] = ap_ref[...]
    pext_ref[HALO + tm:, :] = apn_ref[...] * not_last
    uext_ref[0:HALO, :] = dup_ref[...] * not_first
    uext_ref[HALO:HALO + tm, :] = du_ref[...]
    uext_ref[HALO + tm:, :] = dun_ref[...] * not_last

    conv = jnp.zeros((tm, W_GRP), F32)
    for j in range(SC_WIDTH):
        conv = conv + pext_ref[pl.ds(HALO - SC_WIDTH // 2 + j, tm), :] * wsc_ref[j:j + 1, :]
    ya = _rms_norm(ab_ref[...] * conv, gain_ref[:, GROUP_A * W_GRP:(GROUP_A + 1) * W_GRP])
    first = HALO - CFM_WIDTH // 2
    span = SUBLANES * ((first + CFM_WIDTH - 1) // SUBLANES)
    assert shift_ref.shape[0] == tm + span and span + SUBLANES <= 2 * HALO
    u = jnp.zeros((tm, W_GRP), F32)
    for r in range(SUBLANES):
        shift_ref[...] = uext_ref[pl.ds(r, tm + span), :]
        for off in range(0, span + 1, SUBLANES):
            j = off + r - first
            if 0 <= j < CFM_WIDTH:
                u = u + shift_ref[pl.ds(off, tm), :] * wcfm_ref[j:j + 1, :]
    un = _layer_norm(u, gcfm_ref[...], bcfm_ref[...])
    yd = _rms_norm(un * _sigmoid(un), gain_ref[:, GROUP_D * W_GRP:(GROUP_D + 1) * W_GRP])

    mix = (_dot(ya.astype(BF16), wout_ref[0:W_GRP, :])
           + _dot(yb_ref[...], wout_ref[W_GRP:2 * W_GRP, :])
           + _dot(yc_ref[...], wout_ref[2 * W_GRP:3 * W_GRP, :])
           + _dot(yd.astype(BF16), wout_ref[3 * W_GRP:4 * W_GRP, :]))
    out_ref[...] = _layer_norm(DN_ALPHA * h_ref[...] + mix, g1_ref[...], b1_ref[...])


def _halo_specs(tm, cols, halo, total_rows):
    per = tm // halo
    last = total_rows // halo - 1
    prev = pl.BlockSpec((halo, cols), lambda i: (jnp.maximum(i * per - 1, 0), 0))
    nxt = pl.BlockSpec((halo, cols), lambda i: (jnp.minimum((i + 1) * per, last), 0))
    return prev, nxt


def _outproj(ab, ap, du, yb, yc, h, lw, layer, seq, tm):
    t = h.shape[0]
    prev, nxt = _halo_specs(tm, W_GRP, HALO, t)
    g = _row_spec(tm, W_GRP)
    ls = functools.partial(_layer_spec, layer=layer)
    in_specs = [g, g, prev, nxt, g, prev, nxt, g, g, _row_spec(tm, D_MODEL),
                ls((SC_WIDTH, W_GRP)), ls((CFM_WIDTH, W_GRP)), ls((1, W_GRP)), ls((1, W_GRP)),
                ls((1, 4 * W_GRP)), ls((4 * W_GRP, D_MODEL)), ls((1, D_MODEL)), ls((1, D_MODEL))]
    return pl.pallas_call(
        functools.partial(_outproj_kernel, tiles_per_seq=seq // tm), grid=(t // tm,),
        in_specs=in_specs, out_specs=_row_spec(tm, D_MODEL),
        out_shape=jax.ShapeDtypeStruct((t, D_MODEL), F32),
        scratch_shapes=[pltpu.VMEM((tm + 2 * HALO, W_GRP), F32), pltpu.VMEM((tm + 2 * HALO, W_GRP), F32),
                        pltpu.VMEM((tm + 2 * HALO - SUBLANES, W_GRP), F32)],
        compiler_params=_params("parallel"), name="outproj",
    )(ab, ap, ap, ap, du, du, du, yb, yc, h, lw["w_sc_conv"], lw["w_cfm_dw"], lw["g_cfm_ln"],
      lw["b_cfm_ln"], lw["g_branch"], lw["w_out"], lw["ln1_g"], lw["ln1_b"])


def _ffn_kernel(h_ref, hp_ref, hn_ref, wup_ref, wdw_ref, wdown_ref, g2_ref, b2_ref, out_ref,
                hx_ref, act_ref, ug0_ref, uv0_ref, ug1_ref, uv1_ref, *, tiles_per_seq, n_chunks):
    tm = h_ref.shape[0]
    i = pl.program_id(0) % tiles_per_seq
    not_first = (i != 0).astype(F32)
    not_last = (i != tiles_per_seq - 1).astype(F32)
    ext = tm + 2 * FFN_HALO
    hx_ref[0:FFN_HALO, :] = (hp_ref[...] * not_first).astype(BF16)
    hx_ref[FFN_HALO:FFN_HALO + tm, :] = h_ref[...].astype(BF16)
    hx_ref[FFN_HALO + tm:, :] = (hn_ref[...] * not_last).astype(BF16)

    def conv3(u, w):
        prev = pltpu.roll(u, 1, 0)
        nxt = pltpu.roll(u, ext - 1, 0)
        full = prev * w[0:1, :] + u * w[1:2, :] + nxt * w[2:3, :]
        return full[FFN_HALO:FFN_HALO + tm, :]

    bufs = ((ug0_ref, uv0_ref), (ug1_ref, uv1_ref))

    def up_cols(c, half):
        return wup_ref[:, pl.ds(half * D_FF + c * FF_CHUNK, FF_CHUNK)]

    def up(slot, c):
        hx = hx_ref[...]
        bufs[slot][0][...] = _dot(hx, up_cols(c, 0))
        bufs[slot][1][...] = _dot(hx, up_cols(c, 1))

    def activation(slot, c):
        gate = conv3(bufs[slot][0][...], wdw_ref[0, c])
        val = conv3(bufs[slot][1][...], wdw_ref[1, c])
        return (gate * _sigmoid(gate) * val).astype(BF16)

    up(0, 0)
    for c in range(n_chunks):
        if c + 1 < n_chunks:
            up((c + 1) % 2, c + 1)
        act_ref[:, c * FF_CHUNK:(c + 1) * FF_CHUNK] = activation(c % 2, c)
    rb = tm // FFN_OUT_BLOCKS
    for blk in range(FFN_OUT_BLOCKS):
        rows = pl.ds(blk * rb, rb)
        mix = _dot(act_ref[rows, :], wdown_ref[...])
        out_ref[rows, :] = _layer_norm(DN_ALPHA * h_ref[rows, :] + mix, g2_ref[...], b2_ref[...])


def _ffn(h, lw, layer, seq, tm):
    t = h.shape[0]
    n_chunks = D_FF // FF_CHUNK
    prev, nxt = _halo_specs(tm, D_MODEL, FFN_HALO, t)
    ls = functools.partial(_layer_spec, layer=layer)
    in_specs = [_row_spec(tm, D_MODEL), prev, nxt,
                ls((D_MODEL, 2 * D_FF)), ls((2, n_chunks, FFN_CONV_WIDTH, FF_CHUNK)),
                ls((D_FF, D_MODEL)), ls((1, D_MODEL)), ls((1, D_MODEL))]
    ext = tm + 2 * FFN_HALO
    return pl.pallas_call(
        functools.partial(_ffn_kernel, tiles_per_seq=seq // tm, n_chunks=n_chunks), grid=(t // tm,),
        in_specs=in_specs, out_specs=_row_spec(tm, D_MODEL),
        out_shape=jax.ShapeDtypeStruct((t, D_MODEL), F32),
        scratch_shapes=[pltpu.VMEM((ext, D_MODEL), BF16), pltpu.VMEM((tm, D_FF), BF16)]
        + [pltpu.VMEM((ext, FF_CHUNK), F32)] * 4,
        compiler_params=_params("parallel"), name="conv_ffn",
    )(h, h, h, lw["w_ffn_up"], lw["w_ffn_dw"], lw["w_ffn_down"], lw["ln2_g"], lw["ln2_b"])


def _head_blocks(w, per_head, lo, n, value_layout=False):
    depth, rows = w.shape[:2]
    wh = w.reshape(depth, rows, -1, per_head)[..., lo:lo + n]
    blocks = jnp.pad(wh, ((0, 0), (0, 0), (0, 0), (0, HEAD_BLOCK - n)))
    if value_layout:
        shifted = jnp.pad(wh, ((0, 0), (0, 0), (0, 0), (HEAD_BLOCK - n, 0)))
        odd = (jnp.arange(wh.shape[2]) % 2 == 1)[None, None, :, None]
        blocks = jnp.where(odd, shifted, blocks)
    return blocks.reshape(depth, rows, -1)


def _stack_weights(p):
    w_in = p["w_in"]
    depth = w_in.shape[0]
    sizes = (256, 256, 256, 128, 128, 256, 256, 16, 16, 256, 128, 32, 512)
    offs = [0]
    for s in sizes:
        offs.append(offs[-1] + s)
    seg = lambda idx: w_in[:, :, offs[idx]:offs[idx + 1]]
    zeros = lambda n: jnp.zeros((depth, D_MODEL, n), F32)
    misc = [seg(7), seg(8), zeros(ROPE_LO - 2 * GLA_LOWRANK), seg(11), zeros(LANES - ROPE_LO - MLA_ROPE)]
    w_in_p = jnp.concatenate([seg(0), seg(1), seg(2), seg(3), seg(4), seg(5), seg(6), *misc, seg(10), seg(9),
                              seg(12)], axis=2).astype(BF16)
    hk = GLA_HEADS * GLA_DK
    up = p["w_gla_a_up"]
    zk = jnp.zeros((depth, GLA_LOWRANK, hk), F32)
    w_gate = jnp.concatenate([jnp.concatenate([up[:, 0], zk], axis=2), jnp.concatenate([zk, up[:, 1]], axis=2),
                              jnp.zeros((depth, LANES - 2 * GLA_LOWRANK, 2 * hk), F32)], axis=1)
    w_ukv = p["w_mla_ukv"]
    n_ff = D_FF // FF_CHUNK
    w_dw = p["w_ffn_dw"].reshape(depth, FFN_CONV_WIDTH, 2, n_ff, FF_CHUNK).transpose(0, 2, 3, 1, 4)
    row = lambda a: a.reshape(depth, 1, -1)
    return {
        "w_in": w_in_p,
        "w_gate": w_gate.astype(BF16),
        "b_gate": row(p["b_gla_a"]),
        "g_mla_q": row(p["g_mla_q"]),
        "w_uq": _head_blocks(p["w_mla_uq"], MLA_NOPE + MLA_ROPE, 0, MLA_NOPE + MLA_ROPE).astype(BF16),
        "g_mla_kv": row(p["g_mla_kv"]),
        "w_uk": _head_blocks(w_ukv, MLA_NOPE + MLA_V_DIM, 0, MLA_NOPE).astype(BF16),
        "w_uv": _head_blocks(w_ukv, MLA_NOPE + MLA_V_DIM, MLA_NOPE, MLA_V_DIM, value_layout=True).astype(BF16),
        "g_gla_head": row(p["g_gla_head"]),
        "w_sc_conv": p["w_sc_conv"],
        "w_cfm_dw": p["w_cfm_dw"],
        "g_cfm_ln": row(p["g_cfm_ln"]),
        "b_cfm_ln": row(p["b_cfm_ln"]),
        "g_branch": row(p["g_branch"]),
        "w_out": p["w_out"].astype(BF16),
        "ln1_g": row(p["ln1_g"]),
        "ln1_b": row(p["ln1_b"]),
        "w_ffn_up": p["w_ffn_up"].astype(BF16),
        "w_ffn_dw": w_dw,
        "w_ffn_down": p["w_ffn_down"].astype(BF16),
        "ln2_g": row(p["ln2_g"]),
        "ln2_b": row(p["ln2_b"]),
    }


def kernel(x, positions, ln_in_g, ln_in_b, w_in, w_sc_conv, w_gla_a_up, b_gla_a, g_gla_head, g_mla_q, w_mla_uq,
           g_mla_kv, w_mla_ukv, w_cfm_dw, g_cfm_ln, b_cfm_ln, g_branch, w_out, ln1_g, ln1_b, w_ffn_up, w_ffn_dw,
           w_ffn_down, ln2_g, ln2_b):
    p = dict(w_in=w_in, w_sc_conv=w_sc_conv, w_gla_a_up=w_gla_a_up, b_gla_a=b_gla_a, g_gla_head=g_gla_head,
             g_mla_q=g_mla_q, w_mla_uq=w_mla_uq, g_mla_kv=g_mla_kv, w_mla_ukv=w_mla_ukv, w_cfm_dw=w_cfm_dw,
             g_cfm_ln=g_cfm_ln, b_cfm_ln=b_cfm_ln, g_branch=g_branch, w_out=w_out, ln1_g=ln1_g, ln1_b=ln1_b,
             w_ffn_up=w_ffn_up, w_ffn_dw=w_ffn_dw, w_ffn_down=w_ffn_down, ln2_g=ln2_g, ln2_b=ln2_b)
    bsz, seq, d = x.shape
    assert d == D_MODEL and seq % GLA_CHUNK == 0
    tm = min(TOKEN_TILE, seq)
    assert seq % tm == 0 and tm % HALO == 0
    t = bsz * seq
    rope = _rope_tables(positions, tm)
    h = x.reshape(t, d)
    lw = _stack_weights(p)
    for l in range(DEPTH):
        if l == 0:
            h, *mixer_in = _inproj(h, lw, l, rope, tm, input_norm=(ln_in_g, ln_in_b))
        else:
            mixer_in = _inproj(h, lw, l, rope, tm)
        ab, ap, du, q, k, v, gg, laf, lab, qc, kc, vc = mixer_in
        yb = _gla(q, k, v, laf, lab, gg, lw, l, bsz, seq)
        yc = _attention(qc, kc, vc, lw, l, bsz, seq)
        h = _outproj(ab, ap, du, yb, yc, h, lw, l, seq, tm)
        h = _ffn(h, lw, l, seq, min(FFN_TILE, seq))
    return h.reshape(bsz, seq, d)
```

```python
import functools
import math

import jax
import jax.numpy as jnp
from jax import lax
from jax.experimental import pallas as pl
from jax.experimental.pallas import tpu as pltpu

F32 = jnp.float32
BF16 = jnp.bfloat16

D_MODEL = 1024
DEPTH = 2
W_GRP = 256
GROUP_A, GROUP_B, GROUP_C, GROUP_D = range(4)
SC_WIDTH = 3
GLA_HEADS = 4
GLA_DK = 32
GLA_DV = 64
GLA_LOWRANK = 16
GLA_TAU = 16.0
MLA_HEADS = 4
MLA_V_DIM = 64
MLA_NOPE = 64
MLA_ROPE = 32
MLA_Q_RANK = 256
MLA_KV_RANK = 128
ROPE_BASE = 10000.0
CFM_WIDTH = 31
D_FF = 2816
FFN_CONV_WIDTH = 3
DN_ALPHA = (2.0 * DEPTH) ** 0.25
EPS = 1e-5

LANES = 128
SUBLANES = 8
VMEM_LIMIT_BYTES = 56 * 1024 * 1024

TOKEN_TILE = 512
FFN_TILE = 1024
FFN_OUT_BLOCKS = 4
HALO = 16
FFN_HALO = SUBLANES
FF_CHUNK = 256
GLA_CHUNK = 64
GLA_REF_ROW = GLA_CHUNK // 2
GLA_GROUP = 4
ATTN_TQ = 512
ATTN_TK = 2048
HEAD_BLOCK = LANES
ROPE_LO = MLA_NOPE
ROPE_HALF = MLA_ROPE // 2
LOG2E = math.log2(math.e)

COL_SC_B, COL_SC_C, COL_SC_H = 0, 256, 512
COL_GQK, COL_GV, COL_GG = 768, 1024, 1280
COL_MISC_CKV = 1536
COL_CQ = 1792
COL_CFM_A, COL_CFM_G = 2048, 2304
D_IN_PAD = 2560


def _params(*sem):
    return pltpu.CompilerParams(dimension_semantics=sem, vmem_limit_bytes=VMEM_LIMIT_BYTES)


def _dot(a, b):
    return jnp.dot(a, b, preferred_element_type=F32)


def _dot_nt(a, b):
    return lax.dot_general(a, b, (((1,), (1,)), ((), ())), preferred_element_type=F32)


def _dot_tn(a, b):
    return lax.dot_general(a, b, (((0,), (0,)), ((), ())), preferred_element_type=F32)


def _layer_norm(x, g, b):
    mu = jnp.mean(x, axis=-1, keepdims=True)
    xc = x - mu
    var = jnp.mean(xc * xc, axis=-1, keepdims=True)
    return xc * lax.rsqrt(var + EPS) * g + b


def _rms_norm(x, g):
    return x * lax.rsqrt(jnp.mean(x * x, axis=-1, keepdims=True) + EPS) * g


def _sigmoid(x):
    return 1.0 / (1.0 + jnp.exp(-x))


def _log_sigmoid(x):
    return jnp.minimum(x, 0.0) - jnp.log1p(jnp.exp(-jnp.abs(x)))


def _const_spec(shape):
    return pl.BlockSpec(shape, lambda *_: (0,) * len(shape), pipeline_mode=pl.Buffered(1))


def _layer_spec(shape, layer, block=None):
    idx = (layer,) + (tuple(block) if block is not None else (0,) * len(shape))
    return pl.BlockSpec((None,) + tuple(shape), lambda *_: idx, pipeline_mode=pl.Buffered(1))


def _row_spec(rows, cols):
    return pl.BlockSpec((rows, cols), lambda i: (i, 0))


def _rope_kernel(pos_ref, inv_ref, c_ref, s1_ref, s2_ref):
    ang = inv_ref[...] * pos_ref[...].astype(F32)
    frow = lax.broadcasted_iota(jnp.int32, (ROPE_HALF, LANES), 0)
    lane = lax.broadcasted_iota(jnp.int32, (ROPE_HALF, LANES), 1)
    to_first = jnp.where(lane == ROPE_LO + frow, 1.0, 0.0).astype(BF16)
    to_second = jnp.where(lane == ROPE_LO + ROPE_HALF + frow, 1.0, 0.0).astype(BF16)

    def place(x, where_to):
        hi = x.astype(BF16)
        rem = x - hi.astype(F32)
        mid = rem.astype(BF16)
        lo = (rem - mid.astype(F32)).astype(BF16)
        return _dot_tn(hi, where_to) + _dot_tn(mid, where_to) + _dot_tn(lo, where_to)

    cos, sin = jnp.cos(ang), jnp.sin(ang)
    out_lane = lax.broadcasted_iota(jnp.int32, c_ref.shape, 1)
    c_ref[...] = jnp.where(out_lane < ROPE_LO, 1.0, place(cos, to_first + to_second))
    s1_ref[...] = place(sin, to_second)
    s2_ref[...] = -place(sin, to_first)


def _rope_tables(positions, tm):
    t = positions.size
    inv = ROPE_BASE ** (-jnp.arange(0, MLA_ROPE, 2, dtype=F32) / MLA_ROPE)
    out = jax.ShapeDtypeStruct((t, LANES), F32)
    return pl.pallas_call(
        _rope_kernel, grid=(t // tm,),
        in_specs=[pl.BlockSpec((1, tm), lambda i: (0, i)), _const_spec((ROPE_HALF, 1))],
        out_specs=[_row_spec(tm, LANES)] * 3, out_shape=[out] * 3,
        compiler_params=_params("parallel"), name="rope_tables",
    )(positions.reshape(1, t), inv.reshape(ROPE_HALF, 1))


def _value_lanes(head):
    assert 2 * MLA_V_DIM == HEAD_BLOCK
    return (0, MLA_V_DIM) if head % 2 == 0 else (MLA_V_DIM, 0)


def _rope(x, c, s1, s2):
    return (x * c + pltpu.roll(x, ROPE_HALF, 1) * s1
            + pltpu.roll(x, LANES - ROPE_HALF, 1) * s2)


def _inproj_kernel(*refs, pre_norm):
    refs = list(refs)
    x_ref = refs.pop(0)
    if pre_norm:
        lng_ref, lnb_ref = refs.pop(0), refs.pop(0)
    (w_ref, wg_ref, bg_ref, gq_ref, wuq_ref, gkv_ref, wuk_ref, wuv_ref, c_ref, s1_ref, s2_ref) = refs[:11]
    outs = refs[11:]
    if pre_norm:
        hn_ref = outs.pop(0)
        h = _layer_norm(x_ref[...], lng_ref[...], lnb_ref[...])
        hn_ref[...] = h
    else:
        h = x_ref[...]
    (ab_ref, ap_ref, du_ref, q_ref, k_ref, v_ref, gg_ref, laf_ref, lab_ref, qc_ref, kc_ref, vc_ref) = outs
    hb = h.astype(BF16)

    def proj(lo, n):
        return _dot(hb, w_ref[:, lo:lo + n])

    misc_ckv = proj(COL_MISC_CKV, 2 * LANES)
    cq = proj(COL_CQ, MLA_Q_RANK)
    ab_ref[...] = proj(COL_SC_B, W_GRP)
    ap_ref[...] = proj(COL_SC_C, W_GRP) * proj(COL_SC_H, W_GRP)
    du_ref[...] = proj(COL_CFM_A, W_GRP) * _sigmoid(proj(COL_CFM_G, W_GRP))
    misc = misc_ckv[:, :LANES]
    pre = _dot(misc.astype(BF16), wg_ref[...]) + bg_ref[...]
    la = _log_sigmoid(pre) * (1.0 / GLA_TAU)
    laf_ref[...] = la[:, :LANES]
    lab_ref[...] = la[:, LANES:]
    c, s1, s2 = c_ref[...], s1_ref[...], s2_ref[...]
    cqn = _rms_norm(cq, gq_ref[...]).astype(BF16)
    qm = _dot(cqn, wuq_ref[...])
    qscale = (MLA_NOPE + MLA_ROPE) ** -0.5 * LOG2E
    ckvn = _rms_norm(misc_ckv[:, LANES:], gkv_ref[...]).astype(BF16)
    kn = _dot(ckvn, wuk_ref[...])
    vn = _dot(ckvn, wuv_ref[...])
    lane = lax.broadcasted_iota(jnp.int32, misc.shape, 1)
    kr = _rope(jnp.where(lane >= ROPE_LO, misc, 0.0), c, s1, s2)
    for hh in range(MLA_HEADS):
        sl = slice(hh * HEAD_BLOCK, (hh + 1) * HEAD_BLOCK)
        qc_ref[:, sl] = (_rope(qm[:, sl], c, s1, s2) * qscale).astype(BF16)
        kc_ref[:, sl] = (kn[:, sl] + kr).astype(BF16)
        ones_col = jnp.where(lane == _value_lanes(hh)[1], 1.0, 0.0)
        vc_ref[:, sl] = (vn[:, sl] + ones_col).astype(BF16)
    qk = proj(COL_GQK, 2 * LANES)
    q_ref[...] = qk[:, :LANES] * (GLA_DK ** -0.5)
    k_ref[...] = qk[:, LANES:]
    v_ref[...] = proj(COL_GV, W_GRP)
    gg_ref[...] = proj(COL_GG, W_GRP)


def _inproj(h, lw, layer, rope, tm, input_norm=None):
    t = h.shape[0]
    f = lambda n, dt=F32: jax.ShapeDtypeStruct((t, n), dt)
    hb4 = MLA_HEADS * HEAD_BLOCK
    out_shape = [f(W_GRP), f(W_GRP), f(W_GRP), f(LANES), f(LANES), f(W_GRP), f(W_GRP), f(LANES), f(LANES),
                 f(hb4, BF16), f(hb4, BF16), f(hb4, BF16)]
    ls = functools.partial(_layer_spec, layer=layer)
    in_specs = [ls((D_MODEL, D_IN_PAD)), ls((LANES, 2 * LANES)), ls((1, 2 * LANES)),
                ls((1, MLA_Q_RANK)), ls((MLA_Q_RANK, hb4)),
                ls((1, MLA_KV_RANK)), ls((MLA_KV_RANK, hb4)), ls((MLA_KV_RANK, hb4)),
                _row_spec(tm, LANES), _row_spec(tm, LANES), _row_spec(tm, LANES)]
    args = [lw["w_in"], lw["w_gate"], lw["b_gate"], lw["g_mla_q"], lw["w_uq"], lw["g_mla_kv"], lw["w_uk"],
            lw["w_uv"], *rope]
    pre_norm = input_norm is not None
    if pre_norm:
        out_shape.insert(0, f(D_MODEL))
        in_specs = [_const_spec((1, D_MODEL)), _const_spec((1, D_MODEL))] + in_specs
        args = [a.reshape(1, D_MODEL) for a in input_norm] + args
    out_specs = [_row_spec(tm, s.shape[1]) for s in out_shape]
    return pl.pallas_call(
        functools.partial(_inproj_kernel, pre_norm=pre_norm), grid=(t // tm,),
        in_specs=[_row_spec(tm, D_MODEL)] + in_specs, out_specs=out_specs, out_shape=out_shape,
        compiler_params=_params("parallel"), name="inproj",
    )(h, *args)


def _gla_kernel(q_ref, k_ref, v_ref, laf_ref, lab_ref, gg_ref, ghead_ref, gain_ref, out_ref,
                acc_ref, sf_ref, sb_ref, *, seq):
    cs = GLA_CHUNK
    grp = min(GLA_GROUP, seq // cs)
    gr = grp * cs
    n_groups = seq // gr
    hk = GLA_HEADS * GLA_DK
    hv = GLA_HEADS * GLA_DV
    row = lax.broadcasted_iota(jnp.int32, (gr, gr), 0)
    col = lax.broadcasted_iota(jnp.int32, (gr, gr), 1)
    tri = jnp.where((row >= col) & (row // cs == col // cs), 1.0, 0.0).astype(BF16)
    srow = lax.broadcasted_iota(jnp.int32, (GLA_HEADS * cs, cs), 0) % cs
    scol = lax.broadcasted_iota(jnp.int32, (GLA_HEADS * cs, cs), 1)
    keep = (srow >= scol, srow < scol)
    qlane_head = lax.broadcasted_iota(jnp.int32, (cs, hk), 1) // GLA_DK
    olane_head = lax.broadcasted_iota(jnp.int32, (cs, hv), 1) // GLA_DV
    srow_head = lax.broadcasted_iota(jnp.int32, (hk, hv), 0) // GLA_DK
    scol_head = lax.broadcasted_iota(jnp.int32, (hk, hv), 1) // GLA_DV
    eye = (lax.broadcasted_iota(jnp.int32, (hk, hk), 0) == lax.broadcasted_iota(jnp.int32, (hk, hk), 1))

    acc_ref[...] = jnp.zeros_like(acc_ref)
    sf_ref[...] = jnp.zeros_like(sf_ref)
    sb_ref[...] = jnp.zeros_like(sb_ref)

    def decay_phase(g, direction):
        la_ref = (laf_ref, lab_ref)[direction]
        base = pl.multiple_of(g * gr, gr)
        la = la_ref[pl.ds(base, gr), :]
        hi = la.astype(BF16)
        rem = la - hi.astype(F32)
        mid = rem.astype(BF16)
        lo = (rem - mid.astype(F32)).astype(BF16)
        cum = _dot(tri, hi) + _dot(tri, mid) + _dot(tri, lo)
        return base, la, cum

    def chunk_phase(direction, base, la, cum):
        rows = pl.ds(base, gr)
        q, k = q_ref[rows, :], k_ref[rows, :]
        vb = v_ref[rows, :].astype(BF16)
        p, qin, vbs, kv, decay = [], [], [], [], []
        for i in range(grp):
            sl = slice(i * cs, (i + 1) * cs)
            total = cum[(i + 1) * cs - 1:(i + 1) * cs, :]
            b = cum[sl] if direction == 0 else total - cum[sl] + la[sl]
            ref = b[GLA_REF_ROW:GLA_REF_ROW + 1, :]
            qt = q[sl] * jnp.exp(b - ref)
            kt = k[sl] * jnp.exp(ref - b)
            qbd = jnp.concatenate([jnp.where(qlane_head == hh, qt, 0.0) for hh in range(GLA_HEADS)], axis=0)
            scores = _dot_nt(qbd.astype(BF16), kt.astype(BF16))
            p.append(jnp.where(keep[direction], scores, 0.0).astype(BF16))
            qin.append((qbd * jnp.exp(ref)).astype(BF16))
            khat = (kt * jnp.exp(total - ref)).astype(BF16)
            vbs.append(vb[sl])
            kv.append(jnp.where(srow_head == scol_head, _dot_tn(khat, vb[sl]), 0.0))
            decay.append(jnp.sum(jnp.where(eye, jnp.broadcast_to(jnp.exp(total), (hk, hk)), 0.0),
                                 axis=1, keepdims=True))
        return p, qin, vbs, kv, decay

    def state_phase(direction, base, p, qin, vbs, kv, decay):
        s_ref = (sf_ref, sb_ref)[direction]
        state = s_ref[...]
        for i in (range(grp) if direction == 0 else reversed(range(grp))):
            lhs = jnp.concatenate([qin[i], p[i]], axis=1)
            rhs = jnp.concatenate([state.astype(BF16), vbs[i]], axis=0)
            o_all = _dot(lhs, rhs)
            o = jnp.zeros((cs, hv), F32)
            for hh in range(GLA_HEADS):
                o = o + jnp.where(olane_head == hh, o_all[hh * cs:(hh + 1) * cs, :], 0.0)
            acc_ref[pl.ds(base + i * cs, cs), :] += o
            state = state * decay[i] + kv[i]
        s_ref[...] = state

    hrow = lax.broadcasted_iota(jnp.int32, (hv, hv), 0) // GLA_DV
    hcol = lax.broadcasted_iota(jnp.int32, (hv, hv), 1) // GLA_DV
    head_ones = jnp.where(hrow == hcol, 1.0, 0.0).astype(BF16)

    def finish(base):
        rows = pl.ds(base, gr)
        o = acc_ref[rows, :]
        sq = o * o
        hi = sq.astype(BF16)
        lo = (sq - hi.astype(F32)).astype(BF16)
        ms = (_dot(hi, head_ones) + _dot(lo, head_ones)) * (1.0 / GLA_DV)
        on = o * lax.rsqrt(ms + EPS) * ghead_ref[...]
        gg = gg_ref[rows, :]
        yb = on * (gg * _sigmoid(gg))
        out_ref[rows, :] = _rms_norm(yb, gain_ref[...]).astype(BF16)

    def finish_step(n):
        finish(pl.multiple_of(n * gr, gr))
        finish(pl.multiple_of((n_groups - 1 - n) * gr, gr))

    def step(n, finish_previous):
        if finish_previous:
            finish_step(n - 1)
        fwd = decay_phase(n, 0)
        bwd = decay_phase(n_groups - 1 - n, 1)
        fwd_chunks = chunk_phase(0, *fwd)
        bwd_chunks = chunk_phase(1, *bwd)
        state_phase(0, fwd[0], *fwd_chunks)
        state_phase(1, bwd[0], *bwd_chunks)

    assert n_groups % 2 == 0
    half = n_groups // 2
    lax.fori_loop(0, half + 1, lambda n, c: (step(n, False), c)[1], 0)
    lax.fori_loop(half + 1, n_groups, lambda n, c: (step(n, True), c)[1], 0)
    finish_step(n_groups - 1)


def _gla(q, k, v, laf, lab, gg, lw, layer, bsz, seq):
    hk, hv = GLA_HEADS * GLA_DK, GLA_HEADS * GLA_DV
    seq_spec = lambda n: pl.BlockSpec((seq, n), lambda b: (b, 0))
    return pl.pallas_call(
        functools.partial(_gla_kernel, seq=seq), grid=(bsz,),
        in_specs=[seq_spec(hk), seq_spec(hk), seq_spec(hv), seq_spec(hk), seq_spec(hk), seq_spec(hv),
                  _layer_spec((1, hv), layer), _layer_spec((1, W_GRP), layer, block=(0, GROUP_B))],
        out_specs=seq_spec(hv), out_shape=jax.ShapeDtypeStruct((bsz * seq, hv), BF16),
        scratch_shapes=[pltpu.VMEM((seq, hv), F32), pltpu.VMEM((hk, hv), F32), pltpu.VMEM((hk, hv), F32)],
        compiler_params=_params("parallel"), name="gla",
    )(q, k, v, laf, lab, gg, lw["g_gla_head"], lw["g_branch"])


def _attn_kernel(q_ref, k_ref, v_ref, gain_ref, out_ref, s0_ref, s1_ref, *, seq, tk):
    n_tiles = seq // tk
    bufs = (s0_ref, s1_ref)
    tiles = [(hh, j) for hh in range(MLA_HEADS) for j in range(n_tiles)]

    def scores(dst_ref, hh, j):
        sl = slice(hh * HEAD_BLOCK, (hh + 1) * HEAD_BLOCK)
        dst_ref[...] = _dot_nt(q_ref[:, sl], k_ref[j * tk:(j + 1) * tk, sl])

    heads = []
    m = acc = None
    scores(bufs[0], *tiles[0])
    for idx, (hh, j) in enumerate(tiles):
        if idx + 1 < len(tiles):
            scores(bufs[(idx + 1) % 2], *tiles[idx + 1])
        s = bufs[idx % 2][...]
        pv = lambda p: _dot(p.astype(BF16), v_ref[j * tk:(j + 1) * tk, hh * HEAD_BLOCK:(hh + 1) * HEAD_BLOCK])
        m_tile = jnp.max(s, axis=1, keepdims=True)
        if j == 0:
            m = m_tile
            acc = pv(jnp.exp2(s - m))
        else:
            m_new = jnp.maximum(m, m_tile)
            acc = acc * jnp.exp2(m - m_new) + pv(jnp.exp2(s - m_new))
            m = m_new
        if j == n_tiles - 1:
            ones_lane = _value_lanes(hh)[1]
            heads.append(acc / acc[:, ones_lane:ones_lane + 1])
    lane = lax.broadcasted_iota(jnp.int32, heads[0].shape, 1)
    low_half = lane < MLA_V_DIM
    y = jnp.concatenate([jnp.where(low_half, heads[hh], heads[hh + 1]) for hh in range(0, MLA_HEADS, 2)], axis=1)
    out_ref[...] = _rms_norm(y, gain_ref[...]).astype(BF16)


def _attention(qc, kc, vc, lw, layer, bsz, seq):
    tq, tk = min(ATTN_TQ, seq), min(ATTN_TK, seq)
    nq = seq // tq
    hb4 = MLA_HEADS * HEAD_BLOCK
    kv_spec = pl.BlockSpec((seq, hb4), lambda b, i: (b, 0))
    return pl.pallas_call(
        functools.partial(_attn_kernel, seq=seq, tk=tk), grid=(bsz, nq),
        in_specs=[pl.BlockSpec((tq, hb4), lambda b, i: (b * nq + i, 0)), kv_spec, kv_spec,
                  _layer_spec((1, W_GRP), layer, block=(0, GROUP_C))],
        out_specs=pl.BlockSpec((tq, W_GRP), lambda b, i: (b * nq + i, 0)),
        out_shape=jax.ShapeDtypeStruct((bsz * seq, W_GRP), BF16),
        scratch_shapes=[pltpu.VMEM((tq, tk), F32), pltpu.VMEM((tq, tk), F32)],
        compiler_params=_params("parallel", "parallel"), name="mla_attention",
    )(qc, kc, vc, lw["g_branch"])


def _outproj_kernel(ab_ref, ap_ref, app_ref, apn_ref, du_ref, dup_ref, dun_ref, yb_ref, yc_ref, h_ref,
                    wsc_ref, wcfm_ref, gcfm_ref, bcfm_ref, gain_ref, wout_ref, g1_ref, b1_ref,
                    out_ref, pext_ref, uext_ref, shift_ref, *, tiles_per_seq):
    tm = ab_ref.shape[0]
    i = pl.program_id(0) % tiles_per_seq
    not_first = (i != 0).astype(F32)
    not_last = (i != tiles_per_seq - 1).astype(F32)
    pext_ref[0:HALO, :] = app_ref[...] * not_first
    pext_ref[HALO:HALO + tm, :] = ap_ref[...]
    pext_ref[HALO + tm:, :] = apn_ref[...] * not_last
    uext_ref[0:HALO, :] = dup_ref[...] * not_first
    uext_ref[HALO:HALO + tm, :] = du_ref[...]
    uext_ref[HALO + tm:, :] = dun_ref[...] * not_last

    conv = jnp.zeros((tm, W_GRP), F32)
    for j in range(SC_WIDTH):
        conv = conv + pext_ref[pl.ds(HALO - SC_WIDTH // 2 + j, tm), :] * wsc_ref[j:j + 1, :]
    ya = _rms_norm(ab_ref[...] * conv, gain_ref[:, GROUP_A * W_GRP:(GROUP_A + 1) * W_GRP])
    first = HALO - CFM_WIDTH // 2
    span = SUBLANES * ((first + CFM_WIDTH - 1) // SUBLANES)
    assert shift_ref.shape[0] == tm + span and span + SUBLANES <= 2 * HALO
    u = jnp.zeros((tm, W_GRP), F32)
    for r in range(SUBLANES):
        shift_ref[...] = uext_ref[pl.ds(r, tm + span), :]
        for off in range(0, span + 1, SUBLANES):
            j = off + r - first
            if 0 <= j < CFM_WIDTH:
                u = u + shift_ref[pl.ds(off, tm), :] * wcfm_ref[j:j + 1, :]
    un = _layer_norm(u, gcfm_ref[...], bcfm_ref[...])
    yd = _rms_norm(un * _sigmoid(un), gain_ref[:, GROUP_D * W_GRP:(GROUP_D + 1) * W_GRP])

    mix = (_dot(ya.astype(BF16), wout_ref[0:W_GRP, :])
           + _dot(yb_ref[...], wout_ref[W_GRP:2 * W_GRP, :])
           + _dot(yc_ref[...], wout_ref[2 * W_GRP:3 * W_GRP, :])
           + _dot(yd.astype(BF16), wout_ref[3 * W_GRP:4 * W_GRP, :]))
    out_ref[...] = _layer_norm(DN_ALPHA * h_ref[...] + mix, g1_ref[...], b1_ref[...])


def _halo_specs(tm, cols, halo, total_rows):
    per = tm // halo
    last = total_rows // halo - 1
    prev = pl.BlockSpec((halo, cols), lambda i: (jnp.maximum(i * per - 1, 0), 0))
    nxt = pl.BlockSpec((halo, cols), lambda i: (jnp.minimum((i + 1) * per, last), 0))
    return prev, nxt


def _outproj(ab, ap, du, yb, yc, h, lw, layer, seq, tm):
    t = h.shape[0]
    prev, nxt = _halo_specs(tm, W_GRP, HALO, t)
    g = _row_spec(tm, W_GRP)
    ls = functools.partial(_layer_spec, layer=layer)
    in_specs = [g, g, prev, nxt, g, prev, nxt, g, g, _row_spec(tm, D_MODEL),
                ls((SC_WIDTH, W_GRP)), ls((CFM_WIDTH, W_GRP)), ls((1, W_GRP)), ls((1, W_GRP)),
                ls((1, 4 * W_GRP)), ls((4 * W_GRP, D_MODEL)), ls((1, D_MODEL)), ls((1, D_MODEL))]
    return pl.pallas_call(
        functools.partial(_outproj_kernel, tiles_per_seq=seq // tm), grid=(t // tm,),
        in_specs=in_specs, out_specs=_row_spec(tm, D_MODEL),
        out_shape=jax.ShapeDtypeStruct((t, D_MODEL), F32),
        scratch_shapes=[pltpu.VMEM((tm + 2 * HALO, W_GRP), F32), pltpu.VMEM((tm + 2 * HALO, W_GRP), F32),
                        pltpu.VMEM((tm + 2 * HALO - SUBLANES, W_GRP), F32)],
        compiler_params=_params("parallel"), name="outproj",
    )(ab, ap, ap, ap, du, du, du, yb, yc, h, lw["w_sc_conv"], lw["w_cfm_dw"], lw["g_cfm_ln"],
      lw["b_cfm_ln"], lw["g_branch"], lw["w_out"], lw["ln1_g"], lw["ln1_b"])


def _ffn_kernel(h_ref, hp_ref, hn_ref, wup_ref, wdw_ref, wdown_ref, g2_ref, b2_ref, out_ref,
                hx_ref, act_ref, ug0_ref, uv0_ref, ug1_ref, uv1_ref, *, tiles_per_seq, n_chunks):
    tm = h_ref.shape[0]
    i = pl.program_id(0) % tiles_per_seq
    not_first = (i != 0).astype(F32)
    not_last = (i != tiles_per_seq - 1).astype(F32)
    ext = tm + 2 * FFN_HALO
    hx_ref[0:FFN_HALO, :] = (hp_ref[...] * not_first).astype(BF16)
    hx_ref[FFN_HALO:FFN_HALO + tm, :] = h_ref[...].astype(BF16)
    hx_ref[FFN_HALO + tm:, :] = (hn_ref[...] * not_last).astype(BF16)

    def conv3(u, w):
        prev = pltpu.roll(u, 1, 0)
        nxt = pltpu.roll(u, ext - 1, 0)
        full = prev * w[0:1, :] + u * w[1:2, :] + nxt * w[2:3, :]
        return full[FFN_HALO:FFN_HALO + tm, :]

    bufs = ((ug0_ref, uv0_ref), (ug1_ref, uv1_ref))

    def up_cols(c, half):
        return wup_ref[:, pl.ds(half * D_FF + c * FF_CHUNK, FF_CHUNK)]

    def up(slot, c):
        hx = hx_ref[...]
        bufs[slot][0][...] = _dot(hx, up_cols(c, 0))
        bufs[slot][1][...] = _dot(hx, up_cols(c, 1))

    def activation(slot, c):
        gate = conv3(bufs[slot][0][...], wdw_ref[0, c])
        val = conv3(bufs[slot][1][...], wdw_ref[1, c])
        return (gate * _sigmoid(gate) * val).astype(BF16)

    up(0, 0)
    for c in range(n_chunks):
        if c + 1 < n_chunks:
            up((c + 1) % 2, c + 1)
        act_ref[:, c * FF_CHUNK:(c + 1) * FF_CHUNK] = activation(c % 2, c)
    rb = tm // FFN_OUT_BLOCKS
    for blk in range(FFN_OUT_BLOCKS):
        rows = pl.ds(blk * rb, rb)
        mix = _dot(act_ref[rows, :], wdown_ref[...])
        out_ref[rows, :] = _layer_norm(DN_ALPHA * h_ref[rows, :] + mix, g2_ref[...], b2_ref[...])


def _ffn(h, lw, layer, seq, tm):
    t = h.shape[0]
    n_chunks = D_FF // FF_CHUNK
    prev, nxt = _halo_specs(tm, D_MODEL, FFN_HALO, t)
    ls = functools.partial(_layer_spec, layer=layer)
    in_specs = [_row_spec(tm, D_MODEL), prev, nxt,
                ls((D_MODEL, 2 * D_FF)), ls((2, n_chunks, FFN_CONV_WIDTH, FF_CHUNK)),
                ls((D_FF, D_MODEL)), ls((1, D_MODEL)), ls((1, D_MODEL))]
    ext = tm + 2 * FFN_HALO
    return pl.pallas_call(
        functools.partial(_ffn_kernel, tiles_per_seq=seq // tm, n_chunks=n_chunks), grid=(t // tm,),
        in_specs=in_specs, out_specs=_row_spec(tm, D_MODEL),
        out_shape=jax.ShapeDtypeStruct((t, D_MODEL), F32),
        scratch_shapes=[pltpu.VMEM((ext, D_MODEL), BF16), pltpu.VMEM((tm, D_FF), BF16)]
        + [pltpu.VMEM((ext, FF_CHUNK), F32)] * 4,
        compiler_params=_params("parallel"), name="conv_ffn",
    )(h, h, h, lw["w_ffn_up"], lw["w_ffn_dw"], lw["w_ffn_down"], lw["ln2_g"], lw["ln2_b"])


def _head_blocks(w, per_head, lo, n, value_layout=False):
    depth, rows = w.shape[:2]
    wh = w.reshape(depth, rows, -1, per_head)[..., lo:lo + n]
    blocks = jnp.pad(wh, ((0, 0), (0, 0), (0, 0), (0, HEAD_BLOCK - n)))
    if value_layout:
        shifted = jnp.pad(wh, ((0, 0), (0, 0), (0, 0), (HEAD_BLOCK - n, 0)))
        odd = (jnp.arange(wh.shape[2]) % 2 == 1)[None, None, :, None]
        blocks = jnp.where(odd, shifted, blocks)
    return blocks.reshape(depth, rows, -1)


def _stack_weights(p):
    w_in = p["w_in"]
    depth = w_in.shape[0]
    sizes = (256, 256, 256, 128, 128, 256, 256, 16, 16, 256, 128, 32, 512)
    offs = [0]
    for s in sizes:
        offs.append(offs[-1] + s)
    seg = lambda idx: w_in[:, :, offs[idx]:offs[idx + 1]]
    zeros = lambda n: jnp.zeros((depth, D_MODEL, n), F32)
    misc = [seg(7), seg(8), zeros(ROPE_LO - 2 * GLA_LOWRANK), seg(11), zeros(LANES - ROPE_LO - MLA_ROPE)]
    w_in_p = jnp.concatenate([seg(0), seg(1), seg(2), seg(3), seg(4), seg(5), seg(6), *misc, seg(10), seg(9),
                              seg(12)], axis=2).astype(BF16)
    hk = GLA_HEADS * GLA_DK
    up = p["w_gla_a_up"]
    zk = jnp.zeros((depth, GLA_LOWRANK, hk), F32)
    w_gate = jnp.concatenate([jnp.concatenate([up[:, 0], zk], axis=2), jnp.concatenate([zk, up[:, 1]], axis=2),
                              jnp.zeros((depth, LANES - 2 * GLA_LOWRANK, 2 * hk), F32)], axis=1)
    w_ukv = p["w_mla_ukv"]
    n_ff = D_FF // FF_CHUNK
    w_dw = p["w_ffn_dw"].reshape(depth, FFN_CONV_WIDTH, 2, n_ff, FF_CHUNK).transpose(0, 2, 3, 1, 4)
    row = lambda a: a.reshape(depth, 1, -1)
    return {
        "w_in": w_in_p,
        "w_gate": w_gate.astype(BF16),
        "b_gate": row(p["b_gla_a"]),
        "g_mla_q": row(p["g_mla_q"]),
        "w_uq": _head_blocks(p["w_mla_uq"], MLA_NOPE + MLA_ROPE, 0, MLA_NOPE + MLA_ROPE).astype(BF16),
        "g_mla_kv": row(p["g_mla_kv"]),
        "w_uk": _head_blocks(w_ukv, MLA_NOPE + MLA_V_DIM, 0, MLA_NOPE).astype(BF16),
        "w_uv": _head_blocks(w_ukv, MLA_NOPE + MLA_V_DIM, MLA_NOPE, MLA_V_DIM, value_layout=True).astype(BF16),
        "g_gla_head": row(p["g_gla_head"]),
        "w_sc_conv": p["w_sc_conv"],
        "w_cfm_dw": p["w_cfm_dw"],
        "g_cfm_ln": row(p["g_cfm_ln"]),
        "b_cfm_ln": row(p["b_cfm_ln"]),
        "g_branch": row(p["g_branch"]),
        "w_out": p["w_out"].astype(BF16),
        "ln1_g": row(p["ln1_g"]),
        "ln1_b": row(p["ln1_b"]),
        "w_ffn_up": p["w_ffn_up"].astype(BF16),
        "w_ffn_dw": w_dw,
        "w_ffn_down": p["w_ffn_down"].astype(BF16),
        "ln2_g": row(p["ln2_g"]),
        "ln2_b": row(p["ln2_b"]),
    }


def kernel(x, positions, ln_in_g, ln_in_b, w_in, w_sc_conv, w_gla_a_up, b_gla_a, g_gla_head, g_mla_q, w_mla_uq,
           g_mla_kv, w_mla_ukv, w_cfm_dw, g_cfm_ln, b_cfm_ln, g_branch, w_out, ln1_g, ln1_b, w_ffn_up, w_ffn_dw,
           w_ffn_down, ln2_g, ln2_b):
    p = dict(w_in=w_in, w_sc_conv=w_sc_conv, w_gla_a_up=w_gla_a_up, b_gla_a=b_gla_a, g_gla_head=g_gla_head,
             g_mla_q=g_mla_q, w_mla_uq=w_mla_uq, g_mla_kv=g_mla_kv, w_mla_ukv=w_mla_ukv, w_cfm_dw=w_cfm_dw,
             g_cfm_ln=g_cfm_ln, b_cfm_ln=b_cfm_ln, g_branch=g_branch, w_out=w_out, ln1_g=ln1_g, ln1_b=ln1_b,
             w_ffn_up=w_ffn_up, w_ffn_dw=w_ffn_dw, w_ffn_down=w_ffn_down, ln2_g=ln2_g, ln2_b=ln2_b)
    bsz, seq, d = x.shape
    assert d == D_MODEL and seq % GLA_CHUNK == 0
    tm = min(TOKEN_TILE, seq)
    assert seq % tm == 0 and tm % HALO == 0
    t = bsz * seq
    rope = _rope_tables(positions, tm)
    h = x.reshape(t, d)
    lw = _stack_weights(p)
    for l in range(DEPTH):
        if l == 0:
            h, *mixer_in = _inproj(h, lw, l, rope, tm, input_norm=(ln_in_g, ln_in_b))
        else:
            mixer_in = _inproj(h, lw, l, rope, tm)
        ab, ap, du, q, k, v, gg, laf, lab, qc, kc, vc = mixer_in
        yb = _gla(q, k, v, laf, lab, gg, lw, l, bsz, seq)
        yc = _attention(qc, kc, vc, lw, l, bsz, seq)
        h = _outproj(ab, ap, du, yb, yc, h, lw, l, seq, tm)
        h = _ffn(h, lw, l, seq, min(FFN_TILE, seq))
    return h.reshape(bsz, seq, d)
```
